```python
import math
import jax
import jax.numpy as jnp
from jax import lax
import numpy as np

D_MODEL = 2048
BATCH = 2
SEQ = 4096
DEPTH = 4
DEC_BATCH = 32
DEC_SEQ = 8
PAST_LEN = 16384
PAGE_SIZE = 128

N_EVEN = (DEPTH + 1) // 2
N_ODD = DEPTH // 2
HA = 4
DK_A = 128
DV_A = 256
A_QK = HA * DK_A
A_V = HA * DV_A
GATE_CAP = 15.0
CHUNK_A = 64
HB = 8
DK_B = 128
DV_B = 128
GDN_QKV = HB * (2 * DK_B + DV_B)
GDN_V = HB * DV_B
CONV_W = 4
CHUNK_B = 64
IN_AB = 2 * A_QK + 2 * A_V + 2 * HA + GDN_QKV + GDN_V + 2 * HB
MIX_AB = A_V + GDN_V
HQ_C = 32
HKV_C = 8
GROUP = HQ_C // HKV_C
HD_C = 64
QKV_C = (HQ_C + 2 * HKV_C) * HD_C
WINDOW = 128
ROPE_THETA = 10000.0
FFN_DIM = 5504
EPS = 1e-6

kernel_name = 'hybrid_mlstm_gdn_swa_decoder_step'


def rmsnorm(x, w):
    xf = x.astype(jnp.float32)
    y = xf * lax.rsqrt(jnp.mean(xf * xf, axis=-1, keepdims=True) + EPS)
    return (y * w.astype(jnp.float32)).astype(x.dtype)


def l2norm(x):
    xf = x.astype(jnp.float32)
    return xf * lax.rsqrt(jnp.sum(xf * xf, axis=-1, keepdims=True) + EPS)


def softcap(x, cap):
    return cap * jnp.tanh(x / cap)


def swiglu(x, wg, wu, wd):
    return (jax.nn.silu(x @ wg) * (x @ wu)) @ wd


def rope(x, pos):
    half = x.shape[-1] // 2
    inv = ROPE_THETA ** (-jnp.arange(half, dtype=jnp.float32) / half)
    ang = pos.astype(jnp.float32)[:, None] * inv[None, :]
    cos = jnp.cos(ang)[:, None, :]
    sin = jnp.sin(ang)[:, None, :]
    x1 = x[..., :half].astype(jnp.float32)
    x2 = x[..., half:].astype(jnp.float32)
    return jnp.concatenate([x1 * cos - x2 * sin, x2 * cos + x1 * sin], axis=-1).astype(x.dtype)


def to_chunks(t, c):
    b, l, h = t.shape[:3]
    t = t.reshape((b, l // c, c, h) + t.shape[3:])
    return jnp.moveaxis(jnp.swapaxes(t, 2, 3), 1, 0)


def from_chunks(t):
    nc, b, h, c = t.shape[:4]
    return jnp.swapaxes(jnp.moveaxis(t, 0, 1), 2, 3).reshape((b, nc * c, h) + t.shape[4:])


def mlstm_chunked(q, k, v, ig, lf, C0, n0, m0):
    c = math.gcd(q.shape[1], CHUNK_A)
    xs = tuple(to_chunks(t.astype(jnp.float32), c) for t in (q, k, v, ig, lf))
    causal = jnp.tril(jnp.ones((c, c), dtype=bool))

    def step(carry, inp):
        Cp, npv, mp = carry
        qc, kc, vc, ic, fc = inp
        b = jnp.cumsum(fc, axis=-1)
        dmat = jnp.where(causal, b[..., :, None] - b[..., None, :] + ic[..., None, :], -jnp.inf)
        inter = b + mp[..., None]
        m = jnp.maximum(inter, jnp.max(dmat, axis=-1))
        w_intra = jnp.exp(dmat - m[..., None])
        w_inter = jnp.exp(inter - m)
        s = jnp.einsum('bhtd,bhsd->bhts', qc, kc) * w_intra
        num = w_inter[..., None] * jnp.einsum('bhtd,bhde->bhte', qc, Cp) + jnp.einsum('bhts,bhse->bhte', s, vc)
        den = w_inter * jnp.einsum('bhtd,bhd->bht', qc, npv) + jnp.sum(s, axis=-1)
        h = num / jnp.maximum(jnp.abs(den), jnp.exp(-m))[..., None]
        m_new = m[..., -1]
        wk = jnp.exp(b[..., -1:] - b + ic - m_new[..., None])
        decay = jnp.exp(inter[..., -1] - m_new)
        C_new = decay[..., None, None] * Cp + jnp.einsum('bhs,bhsd,bhse->bhde', wk, kc, vc)
        n_new = decay[..., None] * npv + jnp.einsum('bhs,bhsd->bhd', wk, kc)
        return (C_new, n_new, m_new), h

    init = (C0.astype(jnp.float32), n0.astype(jnp.float32), m0.astype(jnp.float32))
    (C1, n1, m1), h = lax.scan(step, init, xs)
    return from_chunks(h), C1, n1, m1


def gdn_chunked(q, k, v, g, beta, S0):
    c = math.gcd(q.shape[1], CHUNK_B)
    q, k, v, g, beta = (to_chunks(t.astype(jnp.float32), c) for t in (q, k, v, g, beta))
    gc = jnp.cumsum(g, axis=-1)
    incl = jnp.tril(jnp.ones((c, c), dtype=bool))
    strict = jnp.tril(jnp.ones((c, c), dtype=bool), -1)
    diff = gc[..., :, None] - gc[..., None, :]
    decay = jnp.where(incl, jnp.exp(jnp.where(incl, diff, 0.0)), 0.0)
    kb = k * beta[..., None]
    a = jnp.where(strict, jnp.einsum('nbhid,nbhjd->nbhij', kb, k) * decay, 0.0)
    rhs = jnp.concatenate([v * beta[..., None], kb * jnp.exp(gc)[..., None]], axis=-1)
    sol = lax.linalg.triangular_solve(jnp.eye(c, dtype=a.dtype) + a, rhs,
                                      left_side=True, lower=True, unit_diagonal=True)
    u, w = sol[..., :DV_B], sol[..., DV_B:]
    attn = jnp.einsum('nbhid,nbhjd->nbhij', q, k) * decay
    qg = q * jnp.exp(gc)[..., None]
    kd = k * jnp.exp(gc[..., -1:] - gc)[..., None]
    glast = jnp.exp(gc[..., -1])

    def step(S, inp):
        u_c, w_c, attn_c, qg_c, kd_c, gl_c = inp
        v_new = u_c - jnp.einsum('bhid,bhde->bhie', w_c, S)
        o = jnp.einsum('bhid,bhde->bhie', qg_c, S) + jnp.einsum('bhij,bhje->bhie', attn_c, v_new)
        S = gl_c[..., None, None] * S + jnp.einsum('bhid,bhie->bhde', kd_c, v_new)
        return S, o

    S1, o = lax.scan(step, S0.astype(jnp.float32), (u, w, attn, qg, kd, glast))
    return from_chunks(o), S1


def short_conv(x, buf, w):
    L = x.shape[1]
    xp = jnp.concatenate([buf.astype(x.dtype), x], axis=1)
    y = xp[:, 0:L] * w[0]
    for j in range(1, CONV_W):
        y = y + xp[:, j:j + L] * w[j]
    return jax.nn.silu(y), xp[:, L:]


def mixer_ab(h, st, w_in, w_out, b_i, b_f, norm_a, conv_w, a_log, dt_bias, norm_b):
    B, L, _ = h.shape
    C0, n0, m0, S0, conv0 = st
    f32 = jnp.float32
    sizes = (A_QK, A_QK, A_V, A_V, HA, HA, GDN_QKV, GDN_V, HB, HB)
    cuts = [sum(sizes[:i + 1]) for i in range(len(sizes) - 1)]
    aq, ak, av, ao, ai, af, bqkv, bz, bb, ba = jnp.split(h @ w_in, cuts, axis=-1)
    q = aq.reshape(B, L, HA, DK_A).astype(f32) * DK_A ** -0.5
    k = ak.reshape(B, L, HA, DK_A).astype(f32)
    v = av.reshape(B, L, HA, DV_A).astype(f32)
    ig = softcap(ai.astype(f32) + b_i.astype(f32), GATE_CAP)
    lf = jax.nn.log_sigmoid(softcap(af.astype(f32) + b_f.astype(f32), GATE_CAP))
    ha, C1, n1, m1 = mlstm_chunked(q, k, v, ig, lf, C0, n0, m0)
    ha = rmsnorm(ha, norm_a) * jax.nn.sigmoid(ao.reshape(B, L, HA, DV_A).astype(f32))
    qkv, conv1 = short_conv(bqkv, conv0, conv_w)
    bq, bk, bv = jnp.split(qkv, [HB * DK_B, 2 * HB * DK_B], axis=-1)
    bq = l2norm(bq.reshape(B, L, HB, DK_B)) * DK_B ** -0.5
    bk = l2norm(bk.reshape(B, L, HB, DK_B))
    bv = bv.reshape(B, L, HB, DV_B).astype(f32)
    beta = jax.nn.sigmoid(bb.astype(f32))
    g = -jnp.exp(a_log.astype(f32)) * jax.nn.softplus(ba.astype(f32) + dt_bias.astype(f32))
    hb, S1 = gdn_chunked(bq, bk, bv, g, beta, S0)
    hb = rmsnorm(hb, norm_b) * jax.nn.silu(bz.reshape(B, L, HB, DV_B).astype(f32))
    mix = jnp.concatenate([ha.reshape(B, L, A_V), hb.reshape(B, L, GDN_V)], axis=-1).astype(h.dtype)
    return mix @ w_out, (C1, n1, m1, S1, conv1)


def sink_attention(q, k, v, qpos, kpos, sinks):
    s = jnp.einsum('bnqhgd,bnkhd->bnhgqk', q, k).astype(jnp.float32) * HD_C ** -0.5
    dist = qpos[:, :, None] - kpos[:, None, :]
    mask = (dist >= 0) & (dist < WINDOW) & (kpos[:, None, :] >= 0)
    s = jnp.where(mask[None, :, None, None], s, -jnp.inf)
    sink = jnp.broadcast_to(sinks.astype(jnp.float32).reshape(HKV_C, GROUP)[None, None, :, :, None, None],
                            s.shape[:-1] + (1,))
    p = jax.nn.softmax(jnp.concatenate([s, sink], axis=-1), axis=-1)[..., :-1]
    return jnp.einsum('bnhgqk,bnkhd->bnqhgd', p.astype(v.dtype), v)


def mixer_c(h, pos, k_buf, v_buf, w_qkv, w_out, sinks):
    B, L, _ = h.shape
    q, k, v = jnp.split(h @ w_qkv, [HQ_C * HD_C, (HQ_C + HKV_C) * HD_C], axis=-1)
    q = rope(q.reshape(B, L, HQ_C, HD_C), pos)
    k = rope(k.reshape(B, L, HKV_C, HD_C), pos)
    v = v.reshape(B, L, HKV_C, HD_C)
    if k_buf is None:
        nb = L // WINDOW
        qb = q.reshape(B, nb, WINDOW, HKV_C, GROUP, HD_C)
        kb = k.reshape(B, nb, WINDOW, HKV_C, HD_C)
        vb = v.reshape(B, nb, WINDOW, HKV_C, HD_C)
        pad = ((0, 0), (1, 0), (0, 0), (0, 0), (0, 0))
        kband = jnp.concatenate([jnp.pad(kb, pad)[:, :-1], kb], axis=2)
        vband = jnp.concatenate([jnp.pad(vb, pad)[:, :-1], vb], axis=2)
        qpos = pos.reshape(nb, WINDOW)
        kpos = jnp.concatenate([qpos - WINDOW, qpos], axis=-1)
        o = sink_attention(qb, kband, vband, qpos, kpos, sinks)
        k_keep, v_keep = k[:, L - WINDOW:], v[:, L - WINDOW:]
    else:
        kcat = jnp.concatenate([k_buf.astype(k.dtype), k], axis=1)
        vcat = jnp.concatenate([v_buf.astype(v.dtype), v], axis=1)
        kpos = PAST_LEN - WINDOW + jnp.arange(WINDOW + L, dtype=jnp.int32)
        o = sink_attention(q.reshape(B, 1, L, HKV_C, GROUP, HD_C), kcat[:, None], vcat[:, None],
                           pos[None], kpos[None], sinks)
        k_keep, v_keep = kcat[:, L:], vcat[:, L:]
    return o.reshape(B, L, HQ_C * HD_C) @ w_out, (k_keep, v_keep)


def run_trunk(x, pos, rec_state, win_state, norm_w, ffn_w_gate, ffn_w_up, ffn_w_down, ab_w_in, ab_w_out,
              mlstm_b_i, mlstm_b_f, mlstm_norm_w, gdn_conv_w, gdn_a_log, gdn_dt_bias, gdn_norm_w,
              c_w_qkv, c_w_out, c_sinks):
    rec_out, win_out = [], []
    for layer in range(DEPTH):
        j = layer // 2
        nw = norm_w[layer]
        x = x + 0.5 * rmsnorm(swiglu(rmsnorm(x, nw[0]), ffn_w_gate[layer, 0], ffn_w_up[layer, 0],
                                     ffn_w_down[layer, 0]), nw[1])
        hn = rmsnorm(x, nw[2])
        if layer % 2 == 0:
            st = tuple(s[j] for s in rec_state)
            mix, st_new = mixer_ab(hn, st, ab_w_in[j], ab_w_out[j], mlstm_b_i[j], mlstm_b_f[j], mlstm_norm_w[j],
                                   gdn_conv_w[j], gdn_a_log[j], gdn_dt_bias[j], gdn_norm_w[j])
            rec_out.append(st_new)
        else:
            kb = None if win_state is None else win_state[0][j]
            vb = None if win_state is None else win_state[1][j]
            mix, st_new = mixer_c(hn, pos, kb, vb, c_w_qkv[j], c_w_out[j], c_sinks[j])
            win_out.append(st_new)
        x = x + rmsnorm(mix, nw[3])
        x = x + 0.5 * rmsnorm(swiglu(rmsnorm(x, nw[4]), ffn_w_gate[layer, 1], ffn_w_up[layer, 1],
                                     ffn_w_down[layer, 1]), nw[5])
    rec_new = tuple(jnp.stack([st[i] for st in rec_out]) for i in range(5))
    win_new = tuple(jnp.stack([st[i] for st in win_out]) for i in range(2))
    return x, rec_new, win_new


def setup_inputs(seed: int = 0) -> dict:
    key = jax.random.key(seed)
    ks = jax.random.split(key, 26)

    def nrm(k, shape, s):
        return s * jax.random.normal(k, shape, jnp.float32)

    dt = jnp.exp(jax.random.uniform(ks[21], (N_EVEN, HB), jnp.float32, math.log(1e-3), math.log(1e-1)))
    return {
        'x_prompt': nrm(ks[0], (BATCH, SEQ, D_MODEL), 1.0),
        'x_sample': nrm(ks[1], (DEC_BATCH, DEC_SEQ, D_MODEL), 1.0),
        'state_mlstm_C': nrm(ks[2], (N_EVEN, DEC_BATCH, HA, DK_A, DV_A), 0.1),
        'state_mlstm_n': nrm(ks[3], (N_EVEN, DEC_BATCH, HA, DK_A), 0.1),
        'state_mlstm_m': nrm(ks[4], (N_EVEN, DEC_BATCH, HA), 1.0),
        'state_gdn_S': nrm(ks[5], (N_EVEN, DEC_BATCH, HB, DK_B, DV_B), 0.1),
        'state_gdn_conv': nrm(ks[6], (N_EVEN, DEC_BATCH, CONV_W - 1, GDN_QKV), 1.0),
        'cache_swa_k': nrm(ks[7], (N_ODD, DEC_BATCH, WINDOW, HKV_C, HD_C), 1.0),
        'cache_swa_v': nrm(ks[8], (N_ODD, DEC_BATCH, WINDOW, HKV_C, HD_C), 1.0),
        'norm_w': 1.0 + nrm(ks[9], (DEPTH, 6, D_MODEL), 0.05),
        'ffn_w_gate': nrm(ks[10], (DEPTH, 2, D_MODEL, FFN_DIM), D_MODEL ** -0.5),
        'ffn_w_up': nrm(ks[11], (DEPTH, 2, D_MODEL, FFN_DIM), D_MODEL ** -0.5),
        'ffn_w_down': nrm(ks[12], (DEPTH, 2, FFN_DIM, D_MODEL), FFN_DIM ** -0.5),
        'ab_w_in': nrm(ks[13], (N_EVEN, D_MODEL, IN_AB), D_MODEL ** -0.5),
        'ab_w_out': nrm(ks[14], (N_EVEN, MIX_AB, D_MODEL), MIX_AB ** -0.5),
        'mlstm_b_i': nrm(ks[15], (N_EVEN, HA), 0.1),
        'mlstm_b_f': 3.0 + nrm(ks[16], (N_EVEN, HA), 0.5),
        'mlstm_norm_w': 1.0 + nrm(ks[17], (N_EVEN, HA, DV_A), 0.05),
        'gdn_conv_w': nrm(ks[18], (N_EVEN, CONV_W, GDN_QKV), 0.5),
        'gdn_a_log': jnp.log(jax.random.uniform(ks[19], (N_EVEN, HB), jnp.float32, 1.0, 16.0)),
        'gdn_dt_bias': dt + jnp.log(-jnp.expm1(-dt)),
        'gdn_norm_w': 1.0 + nrm(ks[20], (N_EVEN, DV_B), 0.05),
        'c_w_qkv': nrm(ks[22], (N_ODD, D_MODEL, QKV_C), D_MODEL ** -0.5),
        'c_w_out': nrm(ks[23], (N_ODD, HQ_C * HD_C, D_MODEL), (HQ_C * HD_C) ** -0.5),
        'c_sinks': nrm(ks[24], (N_ODD, HQ_C), 1.0),
    }


def reference(x_prompt, x_sample, state_mlstm_C, state_mlstm_n, state_mlstm_m, state_gdn_S, state_gdn_conv,
              cache_swa_k, cache_swa_v, norm_w, ffn_w_gate, ffn_w_up, ffn_w_down, ab_w_in, ab_w_out,
              mlstm_b_i, mlstm_b_f, mlstm_norm_w, gdn_conv_w, gdn_a_log, gdn_dt_bias, gdn_norm_w,
              c_w_qkv, c_w_out, c_sinks):
    weights = (norm_w, ffn_w_gate, ffn_w_up, ffn_w_down, ab_w_in, ab_w_out, mlstm_b_i, mlstm_b_f, mlstm_norm_w,
               gdn_conv_w, gdn_a_log, gdn_dt_bias, gdn_norm_w, c_w_qkv, c_w_out, c_sinks)
    f32 = jnp.float32
    B, L = x_prompt.shape[0], x_prompt.shape[1]
    rec0 = (jnp.zeros((N_EVEN, B, HA, DK_A, DV_A), f32), jnp.zeros((N_EVEN, B, HA, DK_A), f32),
            jnp.zeros((N_EVEN, B, HA), f32), jnp.zeros((N_EVEN, B, HB, DK_B, DV_B), f32),
            jnp.zeros((N_EVEN, B, CONV_W - 1, GDN_QKV), x_prompt.dtype))
    pos_p = jnp.arange(L, dtype=jnp.int32)
    pos_s = PAST_LEN + jnp.arange(x_sample.shape[1], dtype=jnp.int32)
    y_prompt, (pC, pn, pm, pS, pconv), (pk, pv) = run_trunk(x_prompt, pos_p, rec0, None, *weights)
    rec_s = (state_mlstm_C, state_mlstm_n, state_mlstm_m, state_gdn_S, state_gdn_conv)
    y_sample, (sC, sn, sm, sS, sconv), (sk, sv) = run_trunk(x_sample, pos_s, rec_s, (cache_swa_k, cache_swa_v),
                                                            *weights)
    return (y_prompt, y_sample, pC, pn, pm, pS, pconv, pk, pv, sC, sn, sm, sS, sconv, sk, sv)
```

```python
import functools

import jax
import jax.numpy as jnp
from jax import lax
from jax.experimental import pallas as pl
from jax.experimental.pallas import tpu as pltpu

F32 = jnp.float32
BF16 = jnp.bfloat16

D_MODEL = 2048
DEPTH = 4
PAST_LEN = 16384
HA, DK_A, DV_A = 4, 128, 256
A_QK, A_V = HA * DK_A, HA * DV_A
GATE_CAP = 15.0
HB, DK_B, DV_B = 8, 128, 128
GDN_QKV, GDN_V = HB * (2 * DK_B + DV_B), HB * DV_B
CONV_W = 4
HQ_C, HKV_C, HD_C = 32, 8, 64
GROUP = HQ_C // HKV_C
WINDOW = 128
ROPE_THETA = 10000.0
FFN_DIM = 5504
EPS = 1e-6

LANES = 128
FFN_PAD = 5632
FFN_TM, FFN_TF = 768, 512
PROJ_TM = 384
CHUNK = 64
OFF_AQ, OFF_AK, OFF_AV, OFF_AO = 0, A_QK, 2 * A_QK, 2 * A_QK + A_V
OFF_BQKV = 2 * A_QK + 2 * A_V
OFF_BZ = OFF_BQKV + GDN_QKV
OFF_GATES = OFF_BZ + GDN_V
IN_AB_PAD = OFF_GATES + LANES
VMEM_LIMIT = 56 * 1024 * 1024


def _rms(x, w):
    return x * lax.rsqrt(jnp.mean(x * x, axis=-1, keepdims=True) + EPS) * w


def _dot(a, b):
    return jnp.dot(a, b, preferred_element_type=F32)


def _dot_nt(a, b):
    return lax.dot_general(a, b, (((1,), (1,)), ((), ())), preferred_element_type=F32)


def _dot_tn(a, b):
    return lax.dot_general(a, b, (((0,), (0,)), ((), ())), preferred_element_type=F32)


def _params(sem):
    return pltpu.CompilerParams(dimension_semantics=sem, vmem_limit_bytes=VMEM_LIMIT)


def _ffn_kernel(x_ref, nwa_ref, nwb_ref, wg_ref, wu_ref, wd_ref, o_ref, h_ref, *, nf):
    f = pl.program_id(1)

    @pl.when(f == 0)
    def _():
        h_ref[...] = _rms(x_ref[...], nwa_ref[...]).astype(BF16)

    h = h_ref[...]
    g = _dot(h, wg_ref[...])
    u = _dot(h, wu_ref[...])
    part = _dot((g * jax.nn.sigmoid(g) * u).astype(BF16), wd_ref[...])

    @pl.when(f == 0)
    def _():
        o_ref[...] = part

    @pl.when(f != 0)
    def _():
        o_ref[...] += part

    @pl.when(f == nf - 1)
    def _():
        o_ref[...] = x_ref[...] + 0.5 * _rms(o_ref[...], nwb_ref[...])


def _ffn(x, nwa, nwb, wg, wu, wd):
    m, d = x.shape
    nf = wg.shape[1] // FFN_TF
    row = pl.BlockSpec((FFN_TM, d), lambda i, f: (i, 0))
    vec = pl.BlockSpec((1, d), lambda i, f: (0, 0))
    wcol = pl.BlockSpec((d, FFN_TF), lambda i, f: (0, f))
    return pl.pallas_call(
        functools.partial(_ffn_kernel, nf=nf),
        grid=(m // FFN_TM, nf),
        in_specs=[row, vec, vec, wcol, wcol, pl.BlockSpec((FFN_TF, d), lambda i, f: (f, 0))],
        out_specs=row,
        out_shape=jax.ShapeDtypeStruct((m, d), F32),
        scratch_shapes=[pltpu.VMEM((FFN_TM, d), BF16)],
        compiler_params=_params(("parallel", "arbitrary")),
        name="ffn",
    )(x, nwa, nwb, wg, wu, wd)


def _proj_in_kernel(x_ref, nw_ref, w_ref, o_ref, h_ref):
    @pl.when(pl.program_id(1) == 0)
    def _():
        h_ref[...] = _rms(x_ref[...], nw_ref[...]).astype(BF16)

    o_ref[...] = _dot(h_ref[...], w_ref[...])


def _proj_in(x, nw, w, tn):
    m, d = x.shape
    n = w.shape[1]
    return pl.pallas_call(
        _proj_in_kernel,
        grid=(m // PROJ_TM, n // tn),
        in_specs=[pl.BlockSpec((PROJ_TM, d), lambda i, j: (i, 0)),
                  pl.BlockSpec((1, d), lambda i, j: (0, 0)),
                  pl.BlockSpec((d, tn), lambda i, j: (0, j))],
        out_specs=pl.BlockSpec((PROJ_TM, tn), lambda i, j: (i, j)),
        out_shape=jax.ShapeDtypeStruct((m, n), F32),
        scratch_shapes=[pltpu.VMEM((PROJ_TM, d), BF16)],
        compiler_params=_params(("parallel", "arbitrary")),
        name="proj_in",
    )(x, nw, w)


def _proj_out_kernel(*refs, n_parts):
    mix_refs, w_refs = refs[:n_parts], refs[n_parts:2 * n_parts]
    x_ref, nw_ref, o_ref = refs[2 * n_parts:]
    y = _dot(mix_refs[0][...], w_refs[0][...])
    for a_ref, w_ref in zip(mix_refs[1:], w_refs[1:]):
        y += _dot(a_ref[...], w_ref[...])
    o_ref[...] = x_ref[...] + _rms(y, nw_ref[...])


def _proj_out(parts, ws, x, nw):
    m, d = x.shape
    n_parts = len(parts)
    in_specs = [pl.BlockSpec((PROJ_TM, p.shape[1]), lambda i: (i, 0)) for p in parts]
    in_specs += [pl.BlockSpec(w.shape, lambda i: (0, 0)) for w in ws]
    in_specs += [pl.BlockSpec((PROJ_TM, d), lambda i: (i, 0)), pl.BlockSpec((1, d), lambda i: (0, 0))]
    return pl.pallas_call(
        functools.partial(_proj_out_kernel, n_parts=n_parts),
        grid=(m // PROJ_TM,),
        in_specs=in_specs,
        out_specs=pl.BlockSpec((PROJ_TM, d), lambda i: (i, 0)),
        out_shape=jax.ShapeDtypeStruct((m, d), F32),
        compiler_params=_params(("parallel",)),
        name="proj_out",
    )(*parts, *ws, x, nw)


def _iota2(c):
    return lax.broadcasted_iota(jnp.int32, (c, c), 0), lax.broadcasted_iota(jnp.int32, (c, c), 1)


def _to_row(col, eye):
    return jnp.sum(jnp.where(eye, col, 0.0), axis=0, keepdims=True)


def _softcap(x, cap):
    return cap * jnp.tanh(x / cap)


def _mlstm_kernel(q_ref, k_ref, v_ref, og_ref, g_ref, c0_ref, n0_ref, m0_ref, bi_ref, bf_ref, nw_ref,
                  mix_ref, c1_ref, n1_ref, m1_ref, *, c):
    @pl.when(pl.program_id(1) == 0)
    def _():
        c1_ref[...] = c0_ref[...]
        n1_ref[...] = n0_ref[...]
        m1_ref[...] = m0_ref[...]

    row, col = _iota2(c)
    eye = row == col
    causal = col <= row
    gates = g_ref[...]
    ig_all = _softcap(gates[:, 0:HA] + bi_ref[...], GATE_CAP)
    lf_all = jax.nn.log_sigmoid(_softcap(gates[:, HA:2 * HA] + bf_ref[...], GATE_CAP))
    for h in range(HA):
        ig_col = ig_all[:, h:h + 1]
        lf_col = lf_all[:, h:h + 1]
        lf_row = _to_row(lf_col, eye)
        b_col = jnp.sum(jnp.where(causal, lf_row, 0.0), axis=1, keepdims=True)
        b_row = _to_row(b_col, eye)
        ig_row = _to_row(ig_col, eye)
        mp = m1_ref[0, :, h:h + 1]
        cp = c1_ref[0, h]
        npv = n1_ref[0, h:h + 1, :]
        q = q_ref[:, h * DK_A:(h + 1) * DK_A] * DK_A ** -0.5
        k = k_ref[:, h * DK_A:(h + 1) * DK_A]
        v = v_ref[:, h * DV_A:(h + 1) * DV_A]
        qb, kb, vb = q.astype(BF16), k.astype(BF16), v.astype(BF16)

        dmat = jnp.where(causal, b_col - b_row + ig_row, -jnp.inf)
        inter = b_col + mp
        m = jnp.maximum(inter, jnp.max(dmat, axis=1, keepdims=True))
        w_intra = jnp.exp(dmat - m)
        w_inter = jnp.exp(inter - m)
        s = _dot_nt(qb, kb) * w_intra
        num = w_inter * _dot(qb, cp.astype(BF16)) + _dot(s.astype(BF16), vb)
        den = w_inter * jnp.sum(q * npv, axis=1, keepdims=True) + jnp.sum(s, axis=1, keepdims=True)
        hh = num / jnp.maximum(jnp.abs(den), jnp.exp(-m))

        m_new = m[c - 1:c, :]
        wk = jnp.exp(b_col[c - 1:c, :] - b_col + ig_col - m_new)
        decay = jnp.exp(inter[c - 1:c, :] - m_new)
        kw = k * wk
        c1_ref[0, h] = decay * cp + _dot_tn(kw.astype(BF16), vb)
        n1_ref[0, h:h + 1, :] = decay * npv + jnp.sum(kw, axis=0, keepdims=True)
        m1_ref[0, :, h:h + 1] = m_new

        y = _rms(hh, nw_ref[h:h + 1, :]) * jax.nn.sigmoid(og_ref[:, h * DV_A:(h + 1) * DV_A])
        mix_ref[:, h * DV_A:(h + 1) * DV_A] = y.astype(mix_ref.dtype)


def _mlstm(p, c0, n0, m0, b_i, b_f, norm_w, *, batch, c, nc, row0, out_dtype):
    rb = row0 // c

    def rows(width, off):
        return pl.BlockSpec((c, width), lambda b, i: (rb + b * nc + i, off // width))

    state = lambda shape: pl.BlockSpec((1,) + shape, lambda b, i: (b,) + (0,) * len(shape))
    small = lambda shape: pl.BlockSpec(shape, lambda b, i: (0,) * len(shape))
    return pl.pallas_call(
        functools.partial(_mlstm_kernel, c=c),
        grid=(batch, nc),
        in_specs=[rows(A_QK, OFF_AQ), rows(A_QK, OFF_AK), rows(A_V, OFF_AV), rows(A_V, OFF_AO),
                  rows(LANES, OFF_GATES),
                  state((HA, DK_A, DV_A)), state((HA, DK_A)), state((1, HA)),
                  small((1, HA)), small((1, HA)), small((HA, DV_A))],
        out_specs=[pl.BlockSpec((c, A_V), lambda b, i: (b * nc + i, 0)),
                   state((HA, DK_A, DV_A)), state((HA, DK_A)), state((1, HA))],
        out_shape=[jax.ShapeDtypeStruct((batch * nc * c, A_V), out_dtype),
                   jax.ShapeDtypeStruct((batch, HA, DK_A, DV_A), F32),
                   jax.ShapeDtypeStruct((batch, HA, DK_A), F32),
                   jax.ShapeDtypeStruct((batch, 1, HA), F32)],
        compiler_params=_params(("parallel", "arbitrary")),
        name="mlstm",
    )(p, p, p, p, p, c0, n0, m0, b_i, b_f, norm_w)


def _unit_lower_inverse_minus_eye(a, c):
    row, col = _iota2(c)
    nb = min(c, 16)
    sh = nb.bit_length() - 1
    ad = jnp.where((row >> sh) == (col >> sh), a, 0.0)
    mm = lambda x, y: _dot(x.astype(BF16), y.astype(BF16))
    n = -ad
    p = ad
    k = 1
    while 2 * k < nb:
        p = mm(p, p)
        n = n + p + mm(n, p)
        k *= 2
    s = nb
    while s < c:
        sh = s.bit_length() - 1
        off = jnp.where(((row >> (sh + 1)) == (col >> (sh + 1))) & ((row >> sh) != (col >> sh)), a, 0.0)
        x = off + mm(n, off)
        n = n - (x + mm(x, n))
        s *= 2
    return n


def _gdn_kernel(x_ref, z_ref, g_ref, s0_ref, conv0_ref, cw_ref, alog_ref, dtb_ref, nw_ref,
                mix_ref, s1_ref, conv1_ref, xp_ref, *, c):
    @pl.when(pl.program_id(1) == 0)
    def _():
        s1_ref[...] = s0_ref[...]
        xp_ref[5:8, :] = conv0_ref[0]

    xp_ref[8:8 + c, :] = x_ref[...]
    y = xp_ref[5:5 + c, :] * cw_ref[0:1, :]
    for j in range(1, CONV_W):
        y = y + xp_ref[5 + j:5 + j + c, :] * cw_ref[j:j + 1, :]
    qkv = y * jax.nn.sigmoid(y)
    tail = xp_ref[5 + c:8 + c, :]
    conv1_ref[0] = tail
    xp_ref[5:8, :] = tail

    row, col = _iota2(c)
    eye = row == col
    incl = col <= row
    strict = col < row
    gates = g_ref[...]
    beta_all = jax.nn.sigmoid(gates[:, 2 * HA:2 * HA + HB])
    g_all = -jnp.exp(alog_ref[...]) * jax.nn.softplus(gates[:, 2 * HA + HB:2 * HA + 2 * HB] + dtb_ref[...])
    for h in range(HB):
        q = qkv[:, h * DK_B:(h + 1) * DK_B]
        k = qkv[:, HB * DK_B + h * DK_B:HB * DK_B + (h + 1) * DK_B]
        v = qkv[:, 2 * HB * DK_B + h * DV_B:2 * HB * DK_B + (h + 1) * DV_B]
        q = q * lax.rsqrt(jnp.sum(q * q, axis=-1, keepdims=True) + EPS) * DK_B ** -0.5
        k = k * lax.rsqrt(jnp.sum(k * k, axis=-1, keepdims=True) + EPS)
        beta = beta_all[:, h:h + 1]
        g_col = g_all[:, h:h + 1]
        gc_col = jnp.sum(jnp.where(incl, _to_row(g_col, eye), 0.0), axis=1, keepdims=True)
        gc_row = _to_row(gc_col, eye)
        decay = jnp.where(incl, jnp.exp(jnp.where(incl, gc_col - gc_row, 0.0)), 0.0)
        egc = jnp.exp(gc_col)
        kbeta = k * beta
        kb16 = k.astype(BF16)
        a = jnp.where(strict, _dot_nt(kbeta.astype(BF16), kb16) * decay, 0.0)
        rhs = jnp.concatenate([v * beta, kbeta * egc], axis=1)
        n = _unit_lower_inverse_minus_eye(a, c)
        sol = rhs + _dot(n.astype(BF16), rhs.astype(BF16))
        u, w = sol[:, :DV_B], sol[:, DV_B:]
        attn = _dot_nt(q.astype(BF16), kb16) * decay
        qg = q * egc
        gc_last = gc_col[c - 1:c, :]
        kd = k * jnp.exp(gc_last - gc_col)
        st = s1_ref[0, h]
        st16 = st.astype(BF16)
        v_new = u - _dot(w.astype(BF16), st16)
        v_new16 = v_new.astype(BF16)
        o = _dot(qg.astype(BF16), st16) + _dot(attn.astype(BF16), v_new16)
        s1_ref[0, h] = jnp.exp(gc_last) * st + _dot_tn(kd.astype(BF16), v_new16)
        zz = z_ref[:, h * DV_B:(h + 1) * DV_B]
        yo = _rms(o, nw_ref[...]) * (zz * jax.nn.sigmoid(zz))
        mix_ref[:, h * DV_B:(h + 1) * DV_B] = yo.astype(mix_ref.dtype)


def _gdn(p, s0, conv0, conv_w, a_log, dt_bias, norm_w, *, batch, c, nc, row0, out_dtype):
    rb = row0 // c

    def rows(width, off):
        return pl.BlockSpec((c, width), lambda b, i: (rb + b * nc + i, off // width))

    state = lambda shape: pl.BlockSpec((1,) + shape, lambda b, i: (b,) + (0,) * len(shape))
    small = lambda shape: pl.BlockSpec(shape, lambda b, i: (0,) * len(shape))
    return pl.pallas_call(
        functools.partial(_gdn_kernel, c=c),
        grid=(batch, nc),
        in_specs=[rows(GDN_QKV, OFF_BQKV), rows(GDN_V, OFF_BZ), rows(LANES, OFF_GATES),
                  state((HB, DK_B, DV_B)), state((CONV_W - 1, GDN_QKV)),
                  small((CONV_W, GDN_QKV)), small((1, HB)), small((1, HB)), small((1, DV_B))],
        out_specs=[pl.BlockSpec((c, GDN_V), lambda b, i: (b * nc + i, 0)),
                   state((HB, DK_B, DV_B)), state((CONV_W - 1, GDN_QKV))],
        out_shape=[jax.ShapeDtypeStruct((batch * nc * c, GDN_V), out_dtype),
                   jax.ShapeDtypeStruct((batch, HB, DK_B, DV_B), F32),
                   jax.ShapeDtypeStruct((batch, CONV_W - 1, GDN_QKV), F32)],
        scratch_shapes=[pltpu.VMEM((c + 8, GDN_QKV), F32)],
        compiler_params=_params(("parallel", "arbitrary")),
        name="gdn",
    )(p, p, p, s0, conv0, conv_w, a_log, dt_bias, norm_w)


def _rope(x, cos, sin, first_half):
    outs = []
    for j in range(x.shape[1] // LANES):
        xs = x[:, j * LANES:(j + 1) * LANES]
        rot = jnp.where(first_half, -pltpu.roll(xs, LANES - HD_C // 2, 1), pltpu.roll(xs, HD_C // 2, 1))
        outs.append(xs * cos + rot * sin)
    return outs[0] if len(outs) == 1 else jnp.concatenate(outs, axis=1)


def _sink_softmax(s, valid, sink):
    s = jnp.where(valid, s * HD_C ** -0.5, -jnp.inf)
    mx = jnp.maximum(jnp.max(s, axis=1, keepdims=True), sink)
    e = jnp.exp(s - mx)
    return e / (jnp.sum(e, axis=1, keepdims=True) + jnp.exp(sink - mx))


def _swa_prompt_kernel(q_ref, k_ref, v_ref, cos_ref, sin_ref, sink_ref, o_ref, kk_ref, vk_ref, kb_ref, vb_ref):
    nblk = pl.program_id(1)
    w = WINDOW

    @pl.when(nblk == 0)
    def _():
        kb_ref[0:w, :] = jnp.zeros((w, kb_ref.shape[1]), BF16)
        vb_ref[0:w, :] = jnp.zeros((w, vb_ref.shape[1]), BF16)

    cos, sin = cos_ref[...], sin_ref[...]
    first_half = (lax.broadcasted_iota(jnp.int32, (w, LANES), 1) & (HD_C - 1)) < HD_C // 2
    q = _rope(q_ref[...], cos, sin, first_half).astype(BF16)
    k = _rope(k_ref[...], cos, sin, first_half)
    v = v_ref[...]
    kk_ref[0] = k
    vk_ref[0] = v
    kb_ref[w:2 * w, :] = k.astype(BF16)
    vb_ref[w:2 * w, :] = v.astype(BF16)
    qi = lax.broadcasted_iota(jnp.int32, (w, 2 * w), 0)
    kj = lax.broadcasted_iota(jnp.int32, (w, 2 * w), 1)
    valid = (kj > qi) & (kj <= qi + w) & ((kj >= w) | (nblk > 0))
    for g in range(HKV_C):
        kg = kb_ref[:, g * HD_C:(g + 1) * HD_C]
        vg = vb_ref[:, g * HD_C:(g + 1) * HD_C]
        for hh in range(GROUP):
            h = g * GROUP + hh
            p = _sink_softmax(_dot_nt(q[:, h * HD_C:(h + 1) * HD_C], kg), valid, sink_ref[:, h:h + 1])
            o_ref[:, h * HD_C:(h + 1) * HD_C] = _dot(p.astype(BF16), vg).astype(o_ref.dtype)
    kb_ref[0:w, :] = kb_ref[w:2 * w, :]
    vb_ref[0:w, :] = vb_ref[w:2 * w, :]


def _swa_prompt(p, cos, sin, sinks, *, batch, seq):
    w = WINDOW
    nb = seq // w
    nq, nkv = HQ_C * HD_C, HKV_C * HD_C
    keep = pl.BlockSpec((1, w, nkv), lambda b, i: (b, 0, 0))
    tab = pl.BlockSpec((w, LANES), lambda b, i: (i, 0))
    return pl.pallas_call(
        _swa_prompt_kernel,
        grid=(batch, nb),
        in_specs=[pl.BlockSpec((w, nq), lambda b, i: (b * nb + i, 0)),
                  pl.BlockSpec((w, nkv), lambda b, i: (b * nb + i, nq // nkv)),
                  pl.BlockSpec((w, nkv), lambda b, i: (b * nb + i, nq // nkv + 1)),
                  tab, tab, pl.BlockSpec((1, HQ_C), lambda b, i: (0, 0))],
        out_specs=[pl.BlockSpec((w, nq), lambda b, i: (b * nb + i, 0)), keep, keep],
        out_shape=[jax.ShapeDtypeStruct((batch * seq, nq), BF16),
                   jax.ShapeDtypeStruct((batch, w, nkv), F32),
                   jax.ShapeDtypeStruct((batch, w, nkv), F32)],
        scratch_shapes=[pltpu.VMEM((2 * w, nkv), BF16), pltpu.VMEM((2 * w, nkv), BF16)],
        compiler_params=_params(("parallel", "arbitrary")),
        name="swa_prompt",
    )(p, p, p, cos, sin, sinks)


def _swa_sample_kernel(q_ref, k_ref, v_ref, kb_ref, vb_ref, cos_ref, sin_ref, sink_ref,
                       o_ref, kk_ref, vk_ref, kc_ref, vc_ref, *, seq):
    w = WINDOW
    span = 2 * w
    cos, sin = cos_ref[...], sin_ref[...]
    first_half = (lax.broadcasted_iota(jnp.int32, (seq, LANES), 1) & (HD_C - 1)) < HD_C // 2
    q = _rope(q_ref[...], cos, sin, first_half).astype(BF16)
    kc_ref[0:w, :] = kb_ref[0]
    vc_ref[0:w, :] = vb_ref[0]
    kc_ref[w:w + seq, :] = _rope(k_ref[...], cos, sin, first_half)
    vc_ref[w:w + seq, :] = v_ref[...]
    kc_ref[w + seq:span, :] = jnp.zeros((span - w - seq, kc_ref.shape[1]), F32)
    vc_ref[w + seq:span, :] = jnp.zeros((span - w - seq, vc_ref.shape[1]), F32)
    kk_ref[0] = kc_ref[seq:seq + w, :]
    vk_ref[0] = vc_ref[seq:seq + w, :]
    kcat, vcat = kc_ref[...].astype(BF16), vc_ref[...].astype(BF16)
    rows = GROUP * seq
    qi = lax.broadcasted_iota(jnp.int32, (rows, span), 0) & (seq - 1)
    kj = lax.broadcasted_iota(jnp.int32, (rows, span), 1)
    valid = (kj > qi) & (kj <= qi + w) & (kj + (PAST_LEN - w) >= 0)
    for g in range(HKV_C):
        heads = range(g * GROUP, (g + 1) * GROUP)
        qg = jnp.concatenate([q[:, h * HD_C:(h + 1) * HD_C] for h in heads], axis=0)
        sink = jnp.concatenate([jnp.broadcast_to(sink_ref[:, h:h + 1], (seq, 1)) for h in heads], axis=0)
        p = _sink_softmax(_dot_nt(qg, kcat[:, g * HD_C:(g + 1) * HD_C]), valid, sink)
        o = _dot(p.astype(BF16), vcat[:, g * HD_C:(g + 1) * HD_C])
        for hh, h in enumerate(heads):
            o_ref[:, h * HD_C:(h + 1) * HD_C] = o[hh * seq:(hh + 1) * seq, :]


def _swa_sample(p, k_buf, v_buf, cos, sin, sinks, *, batch, seq, row0):
    w = WINDOW
    nq, nkv = HQ_C * HD_C, HKV_C * HD_C
    rb = row0 // seq
    keep = pl.BlockSpec((1, w, nkv), lambda b: (b, 0, 0))
    tab = pl.BlockSpec((seq, LANES), lambda b: (0, 0))
    return pl.pallas_call(
        functools.partial(_swa_sample_kernel, seq=seq),
        grid=(batch,),
        in_specs=[pl.BlockSpec((seq, nq), lambda b: (rb + b, 0)),
                  pl.BlockSpec((seq, nkv), lambda b: (rb + b, nq // nkv)),
                  pl.BlockSpec((seq, nkv), lambda b: (rb + b, nq // nkv + 1)),
                  keep, keep, tab, tab, pl.BlockSpec((1, HQ_C), lambda b: (0, 0))],
        out_specs=[pl.BlockSpec((seq, nq), lambda b: (b, 0)), keep, keep],
        out_shape=[jax.ShapeDtypeStruct((batch * seq, nq), F32),
                   jax.ShapeDtypeStruct((batch, w, nkv), F32),
                   jax.ShapeDtypeStruct((batch, w, nkv), F32)],
        scratch_shapes=[pltpu.VMEM((2 * w, nkv), F32), pltpu.VMEM((2 * w, nkv), F32)],
        compiler_params=_params(("parallel",)),
        name="swa_sample",
    )(p, p, p, k_buf, v_buf, cos, sin, sinks)


def _rope_tables(pos):
    half = HD_C // 2
    inv = ROPE_THETA ** (-jnp.arange(half, dtype=F32) / half)
    ang = pos.astype(F32)[:, None] * inv[None, :]
    reps = LANES // half
    return jnp.tile(jnp.cos(ang), (1, reps)), jnp.tile(jnp.sin(ang), (1, reps))


def kernel(x_prompt, x_sample, state_mlstm_C, state_mlstm_n, state_mlstm_m, state_gdn_S, state_gdn_conv,
           cache_swa_k, cache_swa_v, norm_w, ffn_w_gate, ffn_w_up, ffn_w_down, ab_w_in, ab_w_out,
           mlstm_b_i, mlstm_b_f, mlstm_norm_w, gdn_conv_w, gdn_a_log, gdn_dt_bias, gdn_norm_w,
           c_w_qkv, c_w_out, c_sinks):
    bp, lp, d = x_prompt.shape
    bs, ls, _ = x_sample.shape
    mp, ms = bp * lp, bs * ls
    x = jnp.concatenate([x_prompt.reshape(mp, d), x_sample.reshape(ms, d)], axis=0)

    fpad = FFN_PAD - FFN_DIM
    wg = jnp.pad(ffn_w_gate, ((0, 0), (0, 0), (0, 0), (0, fpad))).astype(BF16)
    wu = jnp.pad(ffn_w_up, ((0, 0), (0, 0), (0, 0), (0, fpad))).astype(BF16)
    wd = jnp.pad(ffn_w_down, ((0, 0), (0, 0), (0, fpad), (0, 0))).astype(BF16)
    o_ai = OFF_BQKV
    o_bq = o_ai + 2 * HA
    o_bz = o_bq + GDN_QKV
    o_bb = o_bz + GDN_V
    w_gates = jnp.concatenate([ab_w_in[:, :, o_ai:o_bq], ab_w_in[:, :, o_bb:]], axis=-1)
    w_gates = jnp.pad(w_gates, ((0, 0), (0, 0), (0, LANES - w_gates.shape[-1])))
    w_ab = jnp.concatenate([ab_w_in[:, :, :o_ai], ab_w_in[:, :, o_bq:o_bb], w_gates], axis=-1).astype(BF16)
    w_ab_out = ab_w_out.astype(BF16)
    w_c = c_w_qkv.astype(BF16)
    w_c_out = c_w_out.astype(BF16)

    cos_p, sin_p = _rope_tables(jnp.arange(lp, dtype=jnp.int32))
    cos_s, sin_s = _rope_tables(PAST_LEN + jnp.arange(ls, dtype=jnp.int32))
    nkv = HKV_C * HD_C

    rec_p, rec_s, win_p, win_s = [], [], [], []
    for layer in range(DEPTH):
        j = layer // 2
        nw = norm_w[layer][:, None, :]
        x = _ffn(x, nw[0], nw[1], wg[layer, 0], wu[layer, 0], wd[layer, 0])
        if layer % 2 == 0:
            p = _proj_in(x, nw[2], w_ab[j], IN_AB_PAD // 3)
            b_i, b_f = mlstm_b_i[j][None, :], mlstm_b_f[j][None, :]
            a_log, dt_b, gnw = gdn_a_log[j][None, :], gdn_dt_bias[j][None, :], gdn_norm_w[j][None, :]
            nc = lp // CHUNK
            ma_p, c1, n1, m1 = _mlstm(
                p, jnp.zeros((bp, HA, DK_A, DV_A), F32), jnp.zeros((bp, HA, DK_A), F32),
                jnp.zeros((bp, 1, HA), F32), b_i, b_f, mlstm_norm_w[j],
                batch=bp, c=CHUNK, nc=nc, row0=0, out_dtype=BF16)
            mb_p, s1, cv1 = _gdn(
                p, jnp.zeros((bp, HB, DK_B, DV_B), F32), jnp.zeros((bp, CONV_W - 1, GDN_QKV), F32),
                gdn_conv_w[j], a_log, dt_b, gnw, batch=bp, c=CHUNK, nc=nc, row0=0, out_dtype=BF16)
            rec_p.append((c1, n1, m1.reshape(bp, HA), s1, cv1))
            ma_s, c1, n1, m1 = _mlstm(
                p, state_mlstm_C[j], state_mlstm_n[j], state_mlstm_m[j][:, None, :], b_i, b_f, mlstm_norm_w[j],
                batch=bs, c=ls, nc=1, row0=mp, out_dtype=F32)
            mb_s, s1, cv1 = _gdn(
                p, state_gdn_S[j], state_gdn_conv[j], gdn_conv_w[j], a_log, dt_b, gnw,
                batch=bs, c=ls, nc=1, row0=mp, out_dtype=F32)
            rec_s.append((c1, n1, m1.reshape(bs, HA), s1, cv1))
            parts = [jnp.concatenate([ma_p, ma_s.astype(BF16)], axis=0),
                     jnp.concatenate([mb_p, mb_s.astype(BF16)], axis=0)]
            x = _proj_out(parts, [w_ab_out[j, :A_V], w_ab_out[j, A_V:]], x, nw[3])
        else:
            p = _proj_in(x, nw[2], w_c[j], 1024)
            sinks = c_sinks[j][None, :]
            o_p, kk, vk = _swa_prompt(p, cos_p, sin_p, sinks, batch=bp, seq=lp)
            win_p.append((kk.reshape(bp, WINDOW, HKV_C, HD_C), vk.reshape(bp, WINDOW, HKV_C, HD_C)))
            o_s, kk, vk = _swa_sample(p, cache_swa_k[j].reshape(bs, WINDOW, nkv),
                                      cache_swa_v[j].reshape(bs, WINDOW, nkv), cos_s, sin_s, sinks,
                                      batch=bs, seq=ls, row0=mp)
            win_s.append((kk.reshape(bs, WINDOW, HKV_C, HD_C), vk.reshape(bs, WINDOW, HKV_C, HD_C)))
            x = _proj_out([jnp.concatenate([o_p, o_s.astype(BF16)], axis=0)], [w_c_out[j]], x, nw[3])
        x = _ffn(x, nw[4], nw[5], wg[layer, 1], wu[layer, 1], wd[layer, 1])

    stack = lambda sts, i: jnp.stack([st[i] for st in sts])
    y_prompt = x[:mp].reshape(bp, lp, d)
    y_sample = x[mp:].reshape(bs, ls, d)
    return ((y_prompt, y_sample)
            + tuple(stack(rec_p, i) for i in range(5)) + tuple(stack(win_p, i) for i in range(2))
            + tuple(stack(rec_s, i) for i in range(5)) + tuple(stack(win_s, i) for i in range(2)))
```

```python
import functools

import jax
import jax.numpy as jnp
from jax import lax
from jax.experimental import pallas as pl
from jax.experimental.pallas import tpu as pltpu

F32 = jnp.float32
BF16 = jnp.bfloat16

D_MODEL = 2048
DEPTH = 4
PAST_LEN = 16384
HA, DK_A, DV_A = 4, 128, 256
A_QK, A_V = HA * DK_A, HA * DV_A
GATE_CAP = 15.0
HB, DK_B, DV_B = 8, 128, 128
GDN_QKV, GDN_V = HB * (2 * DK_B + DV_B), HB * DV_B
CONV_W = 4
HQ_C, HKV_C, HD_C = 32, 8, 64
GROUP = HQ_C // HKV_C
WINDOW = 128
ROPE_THETA = 10000.0
FFN_DIM = 5504
EPS = 1e-6

LANES = 128
FFN_PAD = 5632
FFN_TM, FFN_TF = 768, 512
PROJ_TM = 384
CHUNK = 64
SWA_STAGE_HEADS = 8
OFF_AQ, OFF_AK, OFF_AV, OFF_AO = 0, A_QK, 2 * A_QK, 2 * A_QK + A_V
OFF_BQKV = 2 * A_QK + 2 * A_V
OFF_BZ = OFF_BQKV + GDN_QKV
OFF_GATES = OFF_BZ + GDN_V
IN_AB_PAD = OFF_GATES + LANES
VMEM_LIMIT = 56 * 1024 * 1024


def _rms(x, w):
    return x * lax.rsqrt(jnp.mean(x * x, axis=-1, keepdims=True) + EPS) * w


def _dot(a, b):
    return jnp.dot(a, b, preferred_element_type=F32)


def _dot_nt(a, b):
    return lax.dot_general(a, b, (((1,), (1,)), ((), ())), preferred_element_type=F32)


def _dot_tn(a, b):
    return lax.dot_general(a, b, (((0,), (0,)), ((), ())), preferred_element_type=F32)


def _params(sem):
    return pltpu.CompilerParams(dimension_semantics=sem, vmem_limit_bytes=VMEM_LIMIT)


def _ffn_kernel(x_ref, nwa_ref, nwb_ref, wg_ref, wu_ref, wd_ref, o_ref, h_ref, *, nf):
    f = pl.program_id(1)

    @pl.when(f == 0)
    def _():
        h_ref[...] = _rms(x_ref[...], nwa_ref[...]).astype(BF16)

    h = h_ref[...]
    g = _dot(h, wg_ref[...])
    u = _dot(h, wu_ref[...])
    part = _dot((g * jax.nn.sigmoid(g) * u).astype(BF16), wd_ref[...])

    @pl.when(f == 0)
    def _():
        o_ref[...] = part

    @pl.when(f != 0)
    def _():
        o_ref[...] += part

    @pl.when(f == nf - 1)
    def _():
        o_ref[...] = x_ref[...] + 0.5 * _rms(o_ref[...], nwb_ref[...])


def _ffn(x, nwa, nwb, wg, wu, wd):
    m, d = x.shape
    nf = wg.shape[1] // FFN_TF
    row = pl.BlockSpec((FFN_TM, d), lambda i, f: (i, 0))
    vec = pl.BlockSpec((1, d), lambda i, f: (0, 0))
    wcol = pl.BlockSpec((d, FFN_TF), lambda i, f: (0, f))
    return pl.pallas_call(
        functools.partial(_ffn_kernel, nf=nf),
        grid=(m // FFN_TM, nf),
        in_specs=[row, vec, vec, wcol, wcol, pl.BlockSpec((FFN_TF, d), lambda i, f: (f, 0))],
        out_specs=row,
        out_shape=jax.ShapeDtypeStruct((m, d), F32),
        scratch_shapes=[pltpu.VMEM((FFN_TM, d), BF16)],
        compiler_params=_params(("parallel", "arbitrary")),
        name="ffn",
    )(x, nwa, nwb, wg, wu, wd)


def _proj_in_kernel(x_ref, nw_ref, w_ref, o_ref, h_ref):
    @pl.when(pl.program_id(1) == 0)
    def _():
        h_ref[...] = _rms(x_ref[...], nw_ref[...]).astype(BF16)

    o_ref[...] = _dot(h_ref[...], w_ref[...])


def _proj_in(x, nw, w, tn):
    m, d = x.shape
    n = w.shape[1]
    return pl.pallas_call(
        _proj_in_kernel,
        grid=(m // PROJ_TM, n // tn),
        in_specs=[pl.BlockSpec((PROJ_TM, d), lambda i, j: (i, 0)),
                  pl.BlockSpec((1, d), lambda i, j: (0, 0)),
                  pl.BlockSpec((d, tn), lambda i, j: (0, j))],
        out_specs=pl.BlockSpec((PROJ_TM, tn), lambda i, j: (i, j)),
        out_shape=jax.ShapeDtypeStruct((m, n), F32),
        scratch_shapes=[pltpu.VMEM((PROJ_TM, d), BF16)],
        compiler_params=_params(("parallel", "arbitrary")),
        name="proj_in",
    )(x, nw, w)


def _proj_out_kernel(*refs, n_parts):
    mix_refs, w_refs = refs[:n_parts], refs[n_parts:2 * n_parts]
    x_ref, nw_ref, o_ref = refs[2 * n_parts:]
    y = _dot(mix_refs[0][...], w_refs[0][...])
    for a_ref, w_ref in zip(mix_refs[1:], w_refs[1:]):
        y += _dot(a_ref[...], w_ref[...])
    o_ref[...] = x_ref[...] + _rms(y, nw_ref[...])


def _proj_out(parts, ws, x, nw):
    m, d = x.shape
    n_parts = len(parts)
    in_specs = [pl.BlockSpec((PROJ_TM, p.shape[1]), lambda i: (i, 0)) for p in parts]
    in_specs += [pl.BlockSpec(w.shape, lambda i: (0, 0)) for w in ws]
    in_specs += [pl.BlockSpec((PROJ_TM, d), lambda i: (i, 0)), pl.BlockSpec((1, d), lambda i: (0, 0))]
    return pl.pallas_call(
        functools.partial(_proj_out_kernel, n_parts=n_parts),
        grid=(m // PROJ_TM,),
        in_specs=in_specs,
        out_specs=pl.BlockSpec((PROJ_TM, d), lambda i: (i, 0)),
        out_shape=jax.ShapeDtypeStruct((m, d), F32),
        compiler_params=_params(("parallel",)),
        name="proj_out",
    )(*parts, *ws, x, nw)


def _iota2(c):
    return lax.broadcasted_iota(jnp.int32, (c, c), 0), lax.broadcasted_iota(jnp.int32, (c, c), 1)


def _split3(x):
    a = x.astype(BF16)
    r = x - a.astype(F32)
    b = r.astype(BF16)
    return a, b, (r - b.astype(F32)).astype(BF16)


def _mask_dot(mask01, x):
    a, b, d = _split3(x)
    return _dot(mask01, a) + _dot(mask01, b) + _dot(mask01, d)


def _mask_dot_t(x, mask01):
    a, b, d = _split3(x)
    return _dot_tn(a, mask01) + _dot_tn(b, mask01) + _dot_tn(d, mask01)


def _softcap(x, cap):
    return cap * jnp.tanh(x / cap)


def _mlstm_kernel(q_ref, k_ref, v_ref, og_ref, g_ref, c0_ref, n0_ref, m0_ref, bi_ref, bf_ref, nw_ref,
                  mix_ref, c1_ref, n1_ref, m1_ref, *, c):
    @pl.when(pl.program_id(1) == 0)
    def _():
        c1_ref[...] = c0_ref[...]
        n1_ref[...] = n0_ref[...]
        m1_ref[...] = m0_ref[...]

    row, col = _iota2(c)
    causal = col <= row
    eye01 = (row == col).astype(BF16)
    low01 = causal.astype(BF16)
    upp01 = (row <= col).astype(BF16)
    gates = g_ref[...]
    ig_cols = _softcap(gates[:, 0:HA] + bi_ref[...], GATE_CAP)
    lf_cols = jax.nn.log_sigmoid(_softcap(gates[:, HA:2 * HA] + bf_ref[...], GATE_CAP))
    b_cols = _mask_dot(low01, lf_cols)
    b_rows = _mask_dot_t(lf_cols, upp01)
    ig_rows = _mask_dot_t(ig_cols, eye01)
    heads = range(HA)
    q = [q_ref[:, h * DK_A:(h + 1) * DK_A] * DK_A ** -0.5 for h in heads]
    k = [k_ref[:, h * DK_A:(h + 1) * DK_A] for h in heads]
    q16 = [x.astype(BF16) for x in q]
    k16 = [x.astype(BF16) for x in k]
    v16 = [v_ref[:, h * DV_A:(h + 1) * DV_A].astype(BF16) for h in heads]
    cp = [c1_ref[0, h] for h in heads]
    npv = [n1_ref[0, h:h + 1, :] for h in heads]
    qk = [_dot_nt(q16[h], k16[h]) for h in heads]
    qc = [_dot(q16[h], cp[h].astype(BF16)) for h in heads]
    s, s16, kw, m, w_inter, decay = [], [], [], [], [], []
    for h in heads:
        b_col, ig_col = b_cols[:, h:h + 1], ig_cols[:, h:h + 1]
        dmat = jnp.where(causal, b_col - b_rows[h:h + 1, :] + ig_rows[h:h + 1, :], -jnp.inf)
        inter = b_col + m1_ref[0, :, h:h + 1]
        mh = jnp.maximum(inter, jnp.max(dmat, axis=1, keepdims=True))
        sh = qk[h] * jnp.exp(dmat - mh)
        m_new = mh[c - 1:c, :]
        kw.append(k[h] * jnp.exp(b_col[c - 1:c, :] - b_col + ig_col - m_new))
        decay.append(jnp.exp(inter[c - 1:c, :] - m_new))
        w_inter.append(jnp.exp(inter - mh))
        m.append(mh)
        s.append(sh)
        s16.append(sh.astype(BF16))
    sv = [_dot(s16[h], v16[h]) for h in heads]
    upd = [_dot_tn(kw[h].astype(BF16), v16[h]) for h in heads]
    for h in heads:
        num = w_inter[h] * qc[h] + sv[h]
        den = (w_inter[h] * jnp.sum(q[h] * npv[h], axis=1, keepdims=True)
               + jnp.sum(s[h], axis=1, keepdims=True))
        hh = num / jnp.maximum(jnp.abs(den), jnp.exp(-m[h]))
        c1_ref[0, h] = decay[h] * cp[h] + upd[h]
        n1_ref[0, h:h + 1, :] = decay[h] * npv[h] + jnp.sum(kw[h], axis=0, keepdims=True)
        m1_ref[0, :, h:h + 1] = m[h][c - 1:c, :]
        y = _rms(hh, nw_ref[h:h + 1, :]) * jax.nn.sigmoid(og_ref[:, h * DV_A:(h + 1) * DV_A])
        mix_ref[:, h * DV_A:(h + 1) * DV_A] = y.astype(mix_ref.dtype)


def _mlstm(p, c0, n0, m0, b_i, b_f, norm_w, *, batch, c, nc, row0, out_dtype):
    rb = row0 // c

    def rows(width, off):
        return pl.BlockSpec((c, width), lambda b, i: (rb + b * nc + i, off // width))

    state = lambda shape: pl.BlockSpec((1,) + shape, lambda b, i: (b,) + (0,) * len(shape))
    small = lambda shape: pl.BlockSpec(shape, lambda b, i: (0,) * len(shape))
    return pl.pallas_call(
        functools.partial(_mlstm_kernel, c=c),
        grid=(batch, nc),
        in_specs=[rows(A_QK, OFF_AQ), rows(A_QK, OFF_AK), rows(A_V, OFF_AV), rows(A_V, OFF_AO),
                  rows(LANES, OFF_GATES),
                  state((HA, DK_A, DV_A)), state((HA, DK_A)), state((1, HA)),
                  small((1, HA)), small((1, HA)), small((HA, DV_A))],
        out_specs=[pl.BlockSpec((c, A_V), lambda b, i: (b * nc + i, 0)),
                   state((HA, DK_A, DV_A)), state((HA, DK_A)), state((1, HA))],
        out_shape=[jax.ShapeDtypeStruct((batch * nc * c, A_V), out_dtype),
                   jax.ShapeDtypeStruct((batch, HA, DK_A, DV_A), F32),
                   jax.ShapeDtypeStruct((batch, HA, DK_A), F32),
                   jax.ShapeDtypeStruct((batch, 1, HA), F32)],
        compiler_params=_params(("parallel", "arbitrary")),
        name="mlstm",
    )(p, p, p, p, p, c0, n0, m0, b_i, b_f, norm_w)


def _unit_lower_inverses_minus_eye(mats, c):
    row, col = _iota2(c)
    nb = min(c, 16)
    sh = nb.bit_length() - 1
    diag_blk = (row >> sh) == (col >> sh)
    mm = lambda x, y: _dot(x.astype(BF16), y.astype(BF16))
    p = [jnp.where(diag_blk, a, 0.0) for a in mats]
    n = [-x for x in p]
    k = 1
    while 2 * k < nb:
        p = [mm(x, x) for x in p]
        np_ = [mm(x, y) for x, y in zip(n, p)]
        n = [x + y + z for x, y, z in zip(n, p, np_)]
        k *= 2
    s = nb
    while s < c:
        sh = s.bit_length() - 1
        off_blk = ((row >> (sh + 1)) == (col >> (sh + 1))) & ((row >> sh) != (col >> sh))
        off = [jnp.where(off_blk, a, 0.0) for a in mats]
        x = [o + mm(y, o) for o, y in zip(off, n)]
        xn = [mm(y, z) for y, z in zip(x, n)]
        n = [z - (y + w) for z, y, w in zip(n, x, xn)]
        s *= 2
    return n


def _gdn_kernel(x_ref, z_ref, g_ref, s0_ref, conv0_ref, cw_ref, alog_ref, dtb_ref, nw_ref,
                mix_ref, s1_ref, conv1_ref, xp_ref, *, c):
    @pl.when(pl.program_id(1) == 0)
    def _():
        s1_ref[...] = s0_ref[...]
        xp_ref[5:8, :] = conv0_ref[0]

    xp_ref[8:8 + c, :] = x_ref[...]
    y = xp_ref[5:5 + c, :] * cw_ref[0:1, :]
    for j in range(1, CONV_W):
        y = y + xp_ref[5 + j:5 + j + c, :] * cw_ref[j:j + 1, :]
    qkv = y * jax.nn.sigmoid(y)
    tail = xp_ref[5 + c:8 + c, :]
    conv1_ref[0] = tail
    xp_ref[5:8, :] = tail

    row, col = _iota2(c)
    incl = col <= row
    strict = col < row
    gates = g_ref[...]
    beta_cols = jax.nn.sigmoid(gates[:, 2 * HA:2 * HA + HB])
    g_cols = -jnp.exp(alog_ref[...]) * jax.nn.softplus(gates[:, 2 * HA + HB:2 * HA + 2 * HB] + dtb_ref[...])
    gc_cols = _mask_dot(incl.astype(BF16), g_cols)
    gc_rows = _mask_dot_t(g_cols, (row <= col).astype(BF16))
    egc_cols = jnp.exp(gc_cols)
    heads = range(HB)
    q, k, k16, kbeta, rhs, decay = [], [], [], [], [], []
    for h in heads:
        qh = qkv[:, h * DK_B:(h + 1) * DK_B]
        kh = qkv[:, HB * DK_B + h * DK_B:HB * DK_B + (h + 1) * DK_B]
        vh = qkv[:, 2 * HB * DK_B + h * DV_B:2 * HB * DK_B + (h + 1) * DV_B]
        qh = qh * lax.rsqrt(jnp.sum(qh * qh, axis=-1, keepdims=True) + EPS) * DK_B ** -0.5
        kh = kh * lax.rsqrt(jnp.sum(kh * kh, axis=-1, keepdims=True) + EPS)
        beta = beta_cols[:, h:h + 1]
        kb = kh * beta
        q.append(qh)
        k.append(kh)
        k16.append(kh.astype(BF16))
        kbeta.append(kb)
        rhs.append(jnp.concatenate([vh * beta, kb * egc_cols[:, h:h + 1]], axis=1))
        decay.append(jnp.where(incl, jnp.exp(jnp.where(incl, gc_cols[:, h:h + 1] - gc_rows[h:h + 1, :], 0.0)), 0.0))
    kk = [_dot_nt(kbeta[h].astype(BF16), k16[h]) for h in heads]
    qk = [_dot_nt(q[h].astype(BF16), k16[h]) for h in heads]
    n = _unit_lower_inverses_minus_eye([jnp.where(strict, kk[h] * decay[h], 0.0) for h in heads], c)
    sol = [rhs[h] + _dot(n[h].astype(BF16), rhs[h].astype(BF16)) for h in heads]
    st = [s1_ref[0, h] for h in heads]
    st16 = [x.astype(BF16) for x in st]
    ws = [_dot(sol[h][:, DV_B:].astype(BF16), st16[h]) for h in heads]
    qs = [_dot((q[h] * egc_cols[:, h:h + 1]).astype(BF16), st16[h]) for h in heads]
    v_new16 = [(sol[h][:, :DV_B] - ws[h]).astype(BF16) for h in heads]
    av = [_dot((qk[h] * decay[h]).astype(BF16), v_new16[h]) for h in heads]
    upd = []
    for h in heads:
        kd = k[h] * jnp.exp(gc_cols[c - 1:c, h:h + 1] - gc_cols[:, h:h + 1])
        upd.append(_dot_tn(kd.astype(BF16), v_new16[h]))
    for h in heads:
        s1_ref[0, h] = jnp.exp(gc_cols[c - 1:c, h:h + 1]) * st[h] + upd[h]
        zz = z_ref[:, h * DV_B:(h + 1) * DV_B]
        yo = _rms(qs[h] + av[h], nw_ref[...]) * (zz * jax.nn.sigmoid(zz))
        mix_ref[:, h * DV_B:(h + 1) * DV_B] = yo.astype(mix_ref.dtype)


def _gdn(p, s0, conv0, conv_w, a_log, dt_bias, norm_w, *, batch, c, nc, row0, out_dtype):
    rb = row0 // c

    def rows(width, off):
        return pl.BlockSpec((c, width), lambda b, i: (rb + b * nc + i, off // width))

    state = lambda shape: pl.BlockSpec((1,) + shape, lambda b, i: (b,) + (0,) * len(shape))
    small = lambda shape: pl.BlockSpec(shape, lambda b, i: (0,) * len(shape))
    return pl.pallas_call(
        functools.partial(_gdn_kernel, c=c),
        grid=(batch, nc),
        in_specs=[rows(GDN_QKV, OFF_BQKV), rows(GDN_V, OFF_BZ), rows(LANES, OFF_GATES),
                  state((HB, DK_B, DV_B)), state((CONV_W - 1, GDN_QKV)),
                  small((CONV_W, GDN_QKV)), small((1, HB)), small((1, HB)), small((1, DV_B))],
        out_specs=[pl.BlockSpec((c, GDN_V), lambda b, i: (b * nc + i, 0)),
                   state((HB, DK_B, DV_B)), state((CONV_W - 1, GDN_QKV))],
        out_shape=[jax.ShapeDtypeStruct((batch * nc * c, GDN_V), out_dtype),
                   jax.ShapeDtypeStruct((batch, HB, DK_B, DV_B), F32),
                   jax.ShapeDtypeStruct((batch, CONV_W - 1, GDN_QKV), F32)],
        scratch_shapes=[pltpu.VMEM((c + 8, GDN_QKV), F32)],
        compiler_params=_params(("parallel", "arbitrary")),
        name="gdn",
    )(p, p, p, s0, conv0, conv_w, a_log, dt_bias, norm_w)


def _rope(x, cos, sin, first_half):
    outs = []
    for j in range(x.shape[1] // LANES):
        xs = x[:, j * LANES:(j + 1) * LANES]
        rot = jnp.where(first_half, -pltpu.roll(xs, LANES - HD_C // 2, 1), pltpu.roll(xs, HD_C // 2, 1))
        outs.append(xs * cos + rot * sin)
    return outs[0] if len(outs) == 1 else jnp.concatenate(outs, axis=1)


def _sink_softmax_terms(s, valid, sink):
    s = jnp.where(valid, s * HD_C ** -0.5, -jnp.inf)
    mx = jnp.maximum(jnp.max(s, axis=1, keepdims=True), sink)
    e = jnp.exp(s - mx)
    return e, 1.0 / (jnp.sum(e, axis=1, keepdims=True) + jnp.exp(sink - mx))


def _swa_prompt_kernel(q_ref, k_ref, v_ref, cos_ref, sin_ref, sink_ref, o_ref, kk_ref, vk_ref, kb_ref, vb_ref):
    nblk = pl.program_id(1)
    w = WINDOW

    @pl.when(nblk == 0)
    def _():
        kb_ref[0:w, :] = jnp.zeros((w, kb_ref.shape[1]), BF16)
        vb_ref[0:w, :] = jnp.zeros((w, vb_ref.shape[1]), BF16)

    cos, sin = cos_ref[...], sin_ref[...]
    first_half = (lax.broadcasted_iota(jnp.int32, (w, LANES), 1) & (HD_C - 1)) < HD_C // 2
    q = _rope(q_ref[...], cos, sin, first_half).astype(BF16)
    k = _rope(k_ref[...], cos, sin, first_half)
    v = v_ref[...]
    kk_ref[0] = k
    vk_ref[0] = v
    kb_ref[w:2 * w, :] = k.astype(BF16)
    vb_ref[w:2 * w, :] = v.astype(BF16)
    qi = lax.broadcasted_iota(jnp.int32, (w, 2 * w), 0)
    kj = lax.broadcasted_iota(jnp.int32, (w, 2 * w), 1)
    valid = (kj > qi) & (kj <= qi + w) & ((kj >= w) | (nblk > 0))
    kv = lambda ref, h: ref[:, (h // GROUP) * HD_C:(h // GROUP + 1) * HD_C]
    for h0 in range(0, HQ_C, SWA_STAGE_HEADS):
        hs = range(h0, h0 + SWA_STAGE_HEADS)
        sc = [_dot_nt(q[:, h * HD_C:(h + 1) * HD_C], kv(kb_ref, h)) for h in hs]
        terms = [_sink_softmax_terms(sc[i], valid, sink_ref[:, h:h + 1]) for i, h in enumerate(hs)]
        pv = [_dot(terms[i][0].astype(BF16), kv(vb_ref, h)) for i, h in enumerate(hs)]
        for i, h in enumerate(hs):
            o_ref[:, h * HD_C:(h + 1) * HD_C] = (pv[i] * terms[i][1]).astype(o_ref.dtype)
    kb_ref[0:w, :] = kb_ref[w:2 * w, :]
    vb_ref[0:w, :] = vb_ref[w:2 * w, :]


def _swa_prompt(p, cos, sin, sinks, *, batch, seq):
    w = WINDOW
    nb = seq // w
    nq, nkv = HQ_C * HD_C, HKV_C * HD_C
    keep = pl.BlockSpec((1, w, nkv), lambda b, i: (b, 0, 0))
    tab = pl.BlockSpec((w, LANES), lambda b, i: (i, 0))
    return pl.pallas_call(
        _swa_prompt_kernel,
        grid=(batch, nb),
        in_specs=[pl.BlockSpec((w, nq), lambda b, i: (b * nb + i, 0)),
                  pl.BlockSpec((w, nkv), lambda b, i: (b * nb + i, nq // nkv)),
                  pl.BlockSpec((w, nkv), lambda b, i: (b * nb + i, nq // nkv + 1)),
                  tab, tab, pl.BlockSpec((1, HQ_C), lambda b, i: (0, 0))],
        out_specs=[pl.BlockSpec((w, nq), lambda b, i: (b * nb + i, 0)), keep, keep],
        out_shape=[jax.ShapeDtypeStruct((batch * seq, nq), BF16),
                   jax.ShapeDtypeStruct((batch, w, nkv), F32),
                   jax.ShapeDtypeStruct((batch, w, nkv), F32)],
        scratch_shapes=[pltpu.VMEM((2 * w, nkv), BF16), pltpu.VMEM((2 * w, nkv), BF16)],
        compiler_params=_params(("parallel", "arbitrary")),
        name="swa_prompt",
    )(p, p, p, cos, sin, sinks)


def _swa_sample_kernel(q_ref, k_ref, v_ref, kb_ref, vb_ref, cos_ref, sin_ref, sink_ref,
                       o_ref, kk_ref, vk_ref, kc_ref, vc_ref, *, seq):
    w = WINDOW
    span = 2 * w
    cos, sin = cos_ref[...], sin_ref[...]
    first_half = (lax.broadcasted_iota(jnp.int32, (seq, LANES), 1) & (HD_C - 1)) < HD_C // 2
    q = _rope(q_ref[...], cos, sin, first_half).astype(BF16)
    kc_ref[0:w, :] = kb_ref[0]
    vc_ref[0:w, :] = vb_ref[0]
    kc_ref[w:w + seq, :] = _rope(k_ref[...], cos, sin, first_half)
    vc_ref[w:w + seq, :] = v_ref[...]
    kc_ref[w + seq:span, :] = jnp.zeros((span - w - seq, kc_ref.shape[1]), F32)
    vc_ref[w + seq:span, :] = jnp.zeros((span - w - seq, vc_ref.shape[1]), F32)
    kk_ref[0] = kc_ref[seq:seq + w, :]
    vk_ref[0] = vc_ref[seq:seq + w, :]
    kcat, vcat = kc_ref[...].astype(BF16), vc_ref[...].astype(BF16)
    rows = GROUP * seq
    qi = lax.broadcasted_iota(jnp.int32, (rows, span), 0) & (seq - 1)
    kj = lax.broadcasted_iota(jnp.int32, (rows, span), 1)
    valid = (kj > qi) & (kj <= qi + w) & (kj + (PAST_LEN - w) >= 0)
    groups = range(HKV_C)
    heads = [range(g * GROUP, (g + 1) * GROUP) for g in groups]
    sc = [_dot_nt(jnp.concatenate([q[:, h * HD_C:(h + 1) * HD_C] for h in heads[g]], axis=0),
                  kcat[:, g * HD_C:(g + 1) * HD_C]) for g in groups]
    terms = []
    for g in groups:
        sink = jnp.concatenate([jnp.broadcast_to(sink_ref[:, h:h + 1], (seq, 1)) for h in heads[g]], axis=0)
        terms.append(_sink_softmax_terms(sc[g], valid, sink))
    pv = [_dot(terms[g][0].astype(BF16), vcat[:, g * HD_C:(g + 1) * HD_C]) for g in groups]
    for g in groups:
        o = pv[g] * terms[g][1]
        for hh, h in enumerate(heads[g]):
            o_ref[:, h * HD_C:(h + 1) * HD_C] = o[hh * seq:(hh + 1) * seq, :]


def _swa_sample(p, k_buf, v_buf, cos, sin, sinks, *, batch, seq, row0):
    w = WINDOW
    nq, nkv = HQ_C * HD_C, HKV_C * HD_C
    rb = row0 // seq
    keep = pl.BlockSpec((1, w, nkv), lambda b: (b, 0, 0))
    tab = pl.BlockSpec((seq, LANES), lambda b: (0, 0))
    return pl.pallas_call(
        functools.partial(_swa_sample_kernel, seq=seq),
        grid=(batch,),
        in_specs=[pl.BlockSpec((seq, nq), lambda b: (rb + b, 0)),
                  pl.BlockSpec((seq, nkv), lambda b: (rb + b, nq // nkv)),
                  pl.BlockSpec((seq, nkv), lambda b: (rb + b, nq // nkv + 1)),
                  keep, keep, tab, tab, pl.BlockSpec((1, HQ_C), lambda b: (0, 0))],
        out_specs=[pl.BlockSpec((seq, nq), lambda b: (b, 0)), keep, keep],
        out_shape=[jax.ShapeDtypeStruct((batch * seq, nq), F32),
                   jax.ShapeDtypeStruct((batch, w, nkv), F32),
                   jax.ShapeDtypeStruct((batch, w, nkv), F32)],
        scratch_shapes=[pltpu.VMEM((2 * w, nkv), F32), pltpu.VMEM((2 * w, nkv), F32)],
        compiler_params=_params(("parallel",)),
        name="swa_sample",
    )(p, p, p, k_buf, v_buf, cos, sin, sinks)


def _rope_tables(pos):
    half = HD_C // 2
    inv = ROPE_THETA ** (-jnp.arange(half, dtype=F32) / half)
    ang = pos.astype(F32)[:, None] * inv[None, :]
    reps = LANES // half
    return jnp.tile(jnp.cos(ang), (1, reps)), jnp.tile(jnp.sin(ang), (1, reps))


def kernel(x_prompt, x_sample, state_mlstm_C, state_mlstm_n, state_mlstm_m, state_gdn_S, state_gdn_conv,
           cache_swa_k, cache_swa_v, norm_w, ffn_w_gate, ffn_w_up, ffn_w_down, ab_w_in, ab_w_out,
           mlstm_b_i, mlstm_b_f, mlstm_norm_w, gdn_conv_w, gdn_a_log, gdn_dt_bias, gdn_norm_w,
           c_w_qkv, c_w_out, c_sinks):
    bp, lp, d = x_prompt.shape
    bs, ls, _ = x_sample.shape
    mp, ms = bp * lp, bs * ls
    x = jnp.concatenate([x_prompt.reshape(mp, d), x_sample.reshape(ms, d)], axis=0)

    fpad = FFN_PAD - FFN_DIM
    wg = jnp.pad(ffn_w_gate, ((0, 0), (0, 0), (0, 0), (0, fpad))).astype(BF16)
    wu = jnp.pad(ffn_w_up, ((0, 0), (0, 0), (0, 0), (0, fpad))).astype(BF16)
    wd = jnp.pad(ffn_w_down, ((0, 0), (0, 0), (0, fpad), (0, 0))).astype(BF16)
    o_ai = OFF_BQKV
    o_bq = o_ai + 2 * HA
    o_bz = o_bq + GDN_QKV
    o_bb = o_bz + GDN_V
    w_gates = jnp.concatenate([ab_w_in[:, :, o_ai:o_bq], ab_w_in[:, :, o_bb:]], axis=-1)
    w_gates = jnp.pad(w_gates, ((0, 0), (0, 0), (0, LANES - w_gates.shape[-1])))
    w_ab = jnp.concatenate([ab_w_in[:, :, :o_ai], ab_w_in[:, :, o_bq:o_bb], w_gates], axis=-1).astype(BF16)
    w_ab_out = ab_w_out.astype(BF16)
    w_c = c_w_qkv.astype(BF16)
    w_c_out = c_w_out.astype(BF16)

    cos_p, sin_p = _rope_tables(jnp.arange(lp, dtype=jnp.int32))
    cos_s, sin_s = _rope_tables(PAST_LEN + jnp.arange(ls, dtype=jnp.int32))
    nkv = HKV_C * HD_C

    rec_p, rec_s, win_p, win_s = [], [], [], []
    for layer in range(DEPTH):
        j = layer // 2
        nw = norm_w[layer][:, None, :]
        x = _ffn(x, nw[0], nw[1], wg[layer, 0], wu[layer, 0], wd[layer, 0])
        if layer % 2 == 0:
            p = _proj_in(x, nw[2], w_ab[j], IN_AB_PAD // 3)
            b_i, b_f = mlstm_b_i[j][None, :], mlstm_b_f[j][None, :]
            a_log, dt_b, gnw = gdn_a_log[j][None, :], gdn_dt_bias[j][None, :], gdn_norm_w[j][None, :]
            nc = lp // CHUNK
            ma_p, c1, n1, m1 = _mlstm(
                p, jnp.zeros((bp, HA, DK_A, DV_A), F32), jnp.zeros((bp, HA, DK_A), F32),
                jnp.zeros((bp, 1, HA), F32), b_i, b_f, mlstm_norm_w[j],
                batch=bp, c=CHUNK, nc=nc, row0=0, out_dtype=BF16)
            mb_p, s1, cv1 = _gdn(
                p, jnp.zeros((bp, HB, DK_B, DV_B), F32), jnp.zeros((bp, CONV_W - 1, GDN_QKV), F32),
                gdn_conv_w[j], a_log, dt_b, gnw, batch=bp, c=CHUNK, nc=nc, row0=0, out_dtype=BF16)
            rec_p.append((c1, n1, m1.reshape(bp, HA), s1, cv1))
            ma_s, c1, n1, m1 = _mlstm(
                p, state_mlstm_C[j], state_mlstm_n[j], state_mlstm_m[j][:, None, :], b_i, b_f, mlstm_norm_w[j],
                batch=bs, c=ls, nc=1, row0=mp, out_dtype=F32)
            mb_s, s1, cv1 = _gdn(
                p, state_gdn_S[j], state_gdn_conv[j], gdn_conv_w[j], a_log, dt_b, gnw,
                batch=bs, c=ls, nc=1, row0=mp, out_dtype=F32)
            rec_s.append((c1, n1, m1.reshape(bs, HA), s1, cv1))
            parts = [jnp.concatenate([ma_p, ma_s.astype(BF16)], axis=0),
                     jnp.concatenate([mb_p, mb_s.astype(BF16)], axis=0)]
            x = _proj_out(parts, [w_ab_out[j, :A_V], w_ab_out[j, A_V:]], x, nw[3])
        else:
            p = _proj_in(x, nw[2], w_c[j], 1024)
            sinks = c_sinks[j][None, :]
            o_p, kk, vk = _swa_prompt(p, cos_p, sin_p, sinks, batch=bp, seq=lp)
            win_p.append((kk.reshape(bp, WINDOW, HKV_C, HD_C), vk.reshape(bp, WINDOW, HKV_C, HD_C)))
            o_s, kk, vk = _swa_sample(p, cache_swa_k[j].reshape(bs, WINDOW, nkv),
                                      cache_swa_v[j].reshape(bs, WINDOW, nkv), cos_s, sin_s, sinks,
                                      batch=bs, seq=ls, row0=mp)
            win_s.append((kk.reshape(bs, WINDOW, HKV_C, HD_C), vk.reshape(bs, WINDOW, HKV_C, HD_C)))
            x = _proj_out([jnp.concatenate([o_p, o_s.astype(BF16)], axis=0)], [w_c_out[j]], x, nw[3])
        x = _ffn(x, nw[4], nw[5], wg[layer, 1], wu[layer, 1], wd[layer, 1])

    stack = lambda sts, i: jnp.stack([st[i] for st in sts])
    y_prompt = x[:mp].reshape(bp, lp, d)
    y_sample = x[mp:].reshape(bs, ls, d)
    return ((y_prompt, y_sample)
            + tuple(stack(rec_p, i) for i in range(5)) + tuple(stack(win_p, i) for i in range(2))
            + tuple(stack(rec_s, i) for i in range(5)) + tuple(stack(win_s, i) for i in range(2)))
```

```python
import functools

import jax
import jax.numpy as jnp
from jax import lax
from jax.experimental import pallas as pl
from jax.experimental.pallas import tpu as pltpu

F32 = jnp.float32
BF16 = jnp.bfloat16

D_MODEL = 2048
DEPTH = 4
PAST_LEN = 16384
HA, DK_A, DV_A = 4, 128, 256
A_QK, A_V = HA * DK_A, HA * DV_A
GATE_CAP = 15.0
HB, DK_B, DV_B = 8, 128, 128
GDN_QKV, GDN_V = HB * (2 * DK_B + DV_B), HB * DV_B
CONV_W = 4
HQ_C, HKV_C, HD_C = 32, 8, 64
GROUP = HQ_C // HKV_C
WINDOW = 128
ROPE_THETA = 10000.0
FFN_DIM = 5504
EPS = 1e-6

LANES = 128
FFN_PAD = 5632
FFN_TM, FFN_TF = 768, 512
PROJ_TM, PROJ_TN = 768, 1536
OUT_TM = 256
CHUNK = 64
SWA_STAGE_KV = 2
OFF_AQ, OFF_AK, OFF_AV, OFF_AO = 0, A_QK, 2 * A_QK, 2 * A_QK + A_V
OFF_BQKV = 2 * A_QK + 2 * A_V
OFF_BZ = OFF_BQKV + GDN_QKV
OFF_GATES = OFF_BZ + GDN_V
IN_AB_PAD = 7680
VMEM_LIMIT = 56 * 1024 * 1024


def _rms(x, w):
    return x * lax.rsqrt(jnp.mean(x * x, axis=-1, keepdims=True) + EPS) * w


def _dot(a, b):
    return jnp.dot(a, b, preferred_element_type=F32)


def _dot_nt(a, b):
    return lax.dot_general(a, b, (((1,), (1,)), ((), ())), preferred_element_type=F32)


def _dot_tn(a, b):
    return lax.dot_general(a, b, (((0,), (0,)), ((), ())), preferred_element_type=F32)


def _params(sem):
    return pltpu.CompilerParams(dimension_semantics=sem, vmem_limit_bytes=VMEM_LIMIT)


def _ffn_kernel(x_ref, nwa_ref, nwb_ref, wg_ref, wu_ref, wd_ref, o_ref, h_ref, *, nf):
    f = pl.program_id(1)

    @pl.when(f == 0)
    def _():
        h_ref[...] = _rms(x_ref[...], nwa_ref[...]).astype(BF16)

    h = h_ref[...]
    g = _dot(h, wg_ref[...])
    u = _dot(h, wu_ref[...])
    part = _dot((g * jax.nn.sigmoid(g) * u).astype(BF16), wd_ref[...])

    @pl.when(f == 0)
    def _():
        o_ref[...] = part

    @pl.when(f != 0)
    def _():
        o_ref[...] += part

    @pl.when(f == nf - 1)
    def _():
        o_ref[...] = x_ref[...] + 0.5 * _rms(o_ref[...], nwb_ref[...])


def _ffn(x, nw, wg, wu, wd, layer, half):
    m, d = x.shape
    nf = wg.shape[-1] // FFN_TF
    pre = layer * 6 + 4 * half
    row = pl.BlockSpec((FFN_TM, d), lambda i, f: (i, 0))
    vec = lambda r: pl.BlockSpec((None, 1, d), lambda i, f: (r, 0, 0))
    wcol = pl.BlockSpec((None, None, d, FFN_TF), lambda i, f: (layer, half, 0, f))
    return pl.pallas_call(
        functools.partial(_ffn_kernel, nf=nf),
        grid=(m // FFN_TM, nf),
        in_specs=[row, vec(pre), vec(pre + 1), wcol, wcol,
                  pl.BlockSpec((None, None, FFN_TF, d), lambda i, f: (layer, half, f, 0))],
        out_specs=row,
        out_shape=jax.ShapeDtypeStruct((m, d), F32),
        scratch_shapes=[pltpu.VMEM((FFN_TM, d), BF16)],
        compiler_params=_params(("parallel", "arbitrary")),
        name="ffn",
    )(x, nw, nw, wg, wu, wd)


def _proj_in_kernel(x_ref, nw_ref, w_ref, o_ref, h_ref):
    @pl.when(pl.program_id(1) == 0)
    def _():
        h_ref[...] = _rms(x_ref[...], nw_ref[...]).astype(BF16)

    o_ref[...] = _dot(h_ref[...], w_ref[...])


def _proj_in(x, nw, nw_row, w, j):
    m, d = x.shape
    n = w.shape[-1]
    return pl.pallas_call(
        _proj_in_kernel,
        grid=(m // PROJ_TM, n // PROJ_TN),
        in_specs=[pl.BlockSpec((PROJ_TM, d), lambda i, k: (i, 0)),
                  pl.BlockSpec((None, 1, d), lambda i, k: (nw_row, 0, 0)),
                  pl.BlockSpec((None, d, PROJ_TN), lambda i, k: (j, 0, k))],
        out_specs=pl.BlockSpec((PROJ_TM, PROJ_TN), lambda i, k: (i, k)),
        out_shape=jax.ShapeDtypeStruct((m, n), F32),
        scratch_shapes=[pltpu.VMEM((PROJ_TM, d), BF16)],
        compiler_params=_params(("parallel", "arbitrary")),
        name="proj_in",
    )(x, nw, w)


def _proj_out_kernel(*refs, n_parts, prompt_tiles):
    p_refs, s_refs, w_refs = refs[:n_parts], refs[n_parts:2 * n_parts], refs[2 * n_parts:3 * n_parts]
    x_ref, nw_ref, o_ref = refs[3 * n_parts:]

    def emit(a_refs):
        y = _dot(a_refs[0][...], w_refs[0][...])
        for a_ref, w_ref in zip(a_refs[1:], w_refs[1:]):
            y += _dot(a_ref[...], w_ref[...])
        o_ref[...] = x_ref[...] + _rms(y, nw_ref[...])

    is_prompt = pl.program_id(0) < prompt_tiles
    pl.when(is_prompt)(lambda: emit(p_refs))
    pl.when(jnp.logical_not(is_prompt))(lambda: emit(s_refs))


def _proj_out(p_parts, s_parts, w, j, x, nw, nw_row):
    m, d = x.shape
    n_parts = len(p_parts)
    mp, ms = p_parts[0].shape[0], s_parts[0].shape[0]
    assert mp % OUT_TM == 0 and ms == OUT_TM and mp + ms == m
    prompt_tiles = mp // OUT_TM
    kw = [p.shape[1] for p in p_parts]
    assert len(set(kw)) == 1 and sum(kw) == w.shape[1]
    in_specs = [pl.BlockSpec((OUT_TM, k), lambda i: (jnp.minimum(i, prompt_tiles - 1), 0)) for k in kw]
    in_specs += [pl.BlockSpec((OUT_TM, k), lambda i: (0, 0)) for k in kw]
    in_specs += [pl.BlockSpec((None, k, d), functools.partial(lambda i, part: (j, part, 0), part=part))
                 for part, k in enumerate(kw)]
    in_specs += [pl.BlockSpec((OUT_TM, d), lambda i: (i, 0)), pl.BlockSpec((None, 1, d), lambda i: (nw_row, 0, 0))]
    return pl.pallas_call(
        functools.partial(_proj_out_kernel, n_parts=n_parts, prompt_tiles=prompt_tiles),
        grid=(m // OUT_TM,),
        in_specs=in_specs,
        out_specs=pl.BlockSpec((OUT_TM, d), lambda i: (i, 0)),
        out_shape=jax.ShapeDtypeStruct((m, d), F32),
        compiler_params=_params(("parallel",)),
        name="proj_out",
    )(*p_parts, *s_parts, *([w] * n_parts), x, nw)


def _iota2(c):
    return lax.broadcasted_iota(jnp.int32, (c, c), 0), lax.broadcasted_iota(jnp.int32, (c, c), 1)


def _split3(x):
    a = x.astype(BF16)
    r = x - a.astype(F32)
    b = r.astype(BF16)
    return a, b, (r - b.astype(F32)).astype(BF16)


def _mask_dot(mask01, x):
    a, b, d = _split3(x)
    return _dot(mask01, a) + _dot(mask01, b) + _dot(mask01, d)


def _mask_dot_t(x, mask01):
    a, b, d = _split3(x)
    return _dot_tn(a, mask01) + _dot_tn(b, mask01) + _dot_tn(d, mask01)


def _softcap(x, cap):
    return cap * jnp.tanh(x / cap)


def _mlstm_kernel(q_ref, k_ref, v_ref, og_ref, g_ref, c0_ref, n0_ref, m0_ref, bi_ref, bf_ref, nw_ref,
                  mix_ref, c1_ref, n1_ref, m1_ref, *, c):
    @pl.when(pl.program_id(1) == 0)
    def _():
        c1_ref[...] = c0_ref[...]
        n1_ref[...] = n0_ref[...]
        m1_ref[...] = m0_ref[...]

    row, col = _iota2(c)
    causal = col <= row
    eye01 = (row == col).astype(BF16)
    low01 = causal.astype(BF16)
    upp01 = (row <= col).astype(BF16)
    gates = g_ref[...]
    ig_cols = _softcap(gates[:, 0:HA] + bi_ref[...], GATE_CAP)
    lf_cols = jax.nn.log_sigmoid(_softcap(gates[:, HA:2 * HA] + bf_ref[...], GATE_CAP))
    b_cols = _mask_dot(low01, lf_cols)
    b_rows = _mask_dot_t(lf_cols, upp01)
    ig_rows = _mask_dot_t(ig_cols, eye01)
    heads = range(HA)
    q = [q_ref[:, h * DK_A:(h + 1) * DK_A] * DK_A ** -0.5 for h in heads]
    k = [k_ref[:, h * DK_A:(h + 1) * DK_A] for h in heads]
    q16 = [x.astype(BF16) for x in q]
    k16 = [x.astype(BF16) for x in k]
    v16 = [v_ref[:, h * DV_A:(h + 1) * DV_A].astype(BF16) for h in heads]
    cp = [c1_ref[0, h] for h in heads]
    npv = [n1_ref[0, h:h + 1, :] for h in heads]
    qk = [_dot_nt(q16[h], k16[h]) for h in heads]
    qc = [_dot(q16[h], cp[h].astype(BF16)) for h in heads]
    s, s16, kw, m, w_inter, decay = [], [], [], [], [], []
    for h in heads:
        b_col, ig_col = b_cols[:, h:h + 1], ig_cols[:, h:h + 1]
        dmat = jnp.where(causal, b_col - b_rows[h:h + 1, :] + ig_rows[h:h + 1, :], -jnp.inf)
        inter = b_col + m1_ref[0, :, h:h + 1]
        mh = jnp.maximum(inter, jnp.max(dmat, axis=1, keepdims=True))
        sh = qk[h] * jnp.exp(dmat - mh)
        m_new = mh[c - 1:c, :]
        kw.append(k[h] * jnp.exp(b_col[c - 1:c, :] - b_col + ig_col - m_new))
        decay.append(jnp.exp(inter[c - 1:c, :] - m_new))
        w_inter.append(jnp.exp(inter - mh))
        m.append(mh)
        s.append(sh)
        s16.append(sh.astype(BF16))
    sv = [_dot(s16[h], v16[h]) for h in heads]
    upd = [_dot_tn(kw[h].astype(BF16), v16[h]) for h in heads]
    for h in heads:
        num = w_inter[h] * qc[h] + sv[h]
        den = (w_inter[h] * jnp.sum(q[h] * npv[h], axis=1, keepdims=True)
               + jnp.sum(s[h], axis=1, keepdims=True))
        hh = num / jnp.maximum(jnp.abs(den), jnp.exp(-m[h]))
        c1_ref[0, h] = decay[h] * cp[h] + upd[h]
        n1_ref[0, h:h + 1, :] = decay[h] * npv[h] + jnp.sum(kw[h], axis=0, keepdims=True)
        m1_ref[0, :, h:h + 1] = m[h][c - 1:c, :]
        y = _rms(hh, nw_ref[h:h + 1, :]) * jax.nn.sigmoid(og_ref[:, h * DV_A:(h + 1) * DV_A])
        mix_ref[:, h * DV_A:(h + 1) * DV_A] = y.astype(mix_ref.dtype)


def _mlstm(p, c0, n0, m0, b_i, b_f, norm_w, *, batch, c, nc, row0, out_dtype):
    rb = row0 // c

    def rows(width, off):
        return pl.BlockSpec((c, width), lambda b, i: (rb + b * nc + i, off // width))

    state = lambda shape: pl.BlockSpec((1,) + shape, lambda b, i: (b,) + (0,) * len(shape))
    small = lambda shape: pl.BlockSpec(shape, lambda b, i: (0,) * len(shape))
    return pl.pallas_call(
        functools.partial(_mlstm_kernel, c=c),
        grid=(batch, nc),
        in_specs=[rows(A_QK, OFF_AQ), rows(A_QK, OFF_AK), rows(A_V, OFF_AV), rows(A_V, OFF_AO),
                  rows(LANES, OFF_GATES),
                  state((HA, DK_A, DV_A)), state((HA, DK_A)), state((1, HA)),
                  small((1, HA)), small((1, HA)), small((HA, DV_A))],
        out_specs=[pl.BlockSpec((c, A_V), lambda b, i: (b * nc + i, 0)),
                   state((HA, DK_A, DV_A)), state((HA, DK_A)), state((1, HA))],
        out_shape=[jax.ShapeDtypeStruct((batch * nc * c, A_V), out_dtype),
                   jax.ShapeDtypeStruct((batch, HA, DK_A, DV_A), F32),
                   jax.ShapeDtypeStruct((batch, HA, DK_A), F32),
                   jax.ShapeDtypeStruct((batch, 1, HA), F32)],
        compiler_params=_params(("parallel", "arbitrary")),
        name="mlstm",
    )(p, p, p, p, p, c0, n0, m0, b_i, b_f, norm_w)


def _unit_lower_inverses_minus_eye(mats, c):
    row, col = _iota2(c)
    nb = min(c, 16)
    sh = nb.bit_length() - 1
    diag_blk = (row >> sh) == (col >> sh)
    mm = lambda x, y: _dot(x.astype(BF16), y.astype(BF16))
    p = [jnp.where(diag_blk, a, 0.0) for a in mats]
    n = [-x for x in p]
    k = 1
    while 2 * k < nb:
        p = [mm(x, x) for x in p]
        np_ = [mm(x, y) for x, y in zip(n, p)]
        n = [x + y + z for x, y, z in zip(n, p, np_)]
        k *= 2
    s = nb
    while s < c:
        sh = s.bit_length() - 1
        off_blk = ((row >> (sh + 1)) == (col >> (sh + 1))) & ((row >> sh) != (col >> sh))
        off = [jnp.where(off_blk, a, 0.0) for a in mats]
        x = [o + mm(y, o) for o, y in zip(off, n)]
        xn = [mm(y, z) for y, z in zip(x, n)]
        n = [z - (y + w) for z, y, w in zip(n, x, xn)]
        s *= 2
    return n


def _gdn_kernel(x_ref, z_ref, g_ref, s0_ref, conv0_ref, cw_ref, alog_ref, dtb_ref, nw_ref,
                mix_ref, s1_ref, conv1_ref, xp_ref, *, c):
    @pl.when(pl.program_id(1) == 0)
    def _():
        s1_ref[...] = s0_ref[...]
        xp_ref[5:8, :] = conv0_ref[0]

    xp_ref[8:8 + c, :] = x_ref[...]
    y = xp_ref[5:5 + c, :] * cw_ref[0:1, :]
    for j in range(1, CONV_W):
        y = y + xp_ref[5 + j:5 + j + c, :] * cw_ref[j:j + 1, :]
    qkv = y * jax.nn.sigmoid(y)
    tail = xp_ref[5 + c:8 + c, :]
    conv1_ref[0] = tail
    xp_ref[5:8, :] = tail

    row, col = _iota2(c)
    incl = col <= row
    strict = col < row
    gates = g_ref[...]
    beta_cols = jax.nn.sigmoid(gates[:, 2 * HA:2 * HA + HB])
    g_cols = -jnp.exp(alog_ref[...]) * jax.nn.softplus(gates[:, 2 * HA + HB:2 * HA + 2 * HB] + dtb_ref[...])
    gc_cols = _mask_dot(incl.astype(BF16), g_cols)
    gc_rows = _mask_dot_t(g_cols, (row <= col).astype(BF16))
    egc_cols = jnp.exp(gc_cols)
    heads = range(HB)
    q, k, k16, kbeta, rhs, decay = [], [], [], [], [], []
    for h in heads:
        qh = qkv[:, h * DK_B:(h + 1) * DK_B]
        kh = qkv[:, HB * DK_B + h * DK_B:HB * DK_B + (h + 1) * DK_B]
        vh = qkv[:, 2 * HB * DK_B + h * DV_B:2 * HB * DK_B + (h + 1) * DV_B]
        qh = qh * lax.rsqrt(jnp.sum(qh * qh, axis=-1, keepdims=True) + EPS) * DK_B ** -0.5
        kh = kh * lax.rsqrt(jnp.sum(kh * kh, axis=-1, keepdims=True) + EPS)
        beta = beta_cols[:, h:h + 1]
        kb = kh * beta
        q.append(qh)
        k.append(kh)
        k16.append(kh.astype(BF16))
        kbeta.append(kb)
        rhs.append(jnp.concatenate([vh * beta, kb * egc_cols[:, h:h + 1]], axis=1))
        decay.append(jnp.where(incl, jnp.exp(jnp.where(incl, gc_cols[:, h:h + 1] - gc_rows[h:h + 1, :], 0.0)), 0.0))
    kk = [_dot_nt(kbeta[h].astype(BF16), k16[h]) for h in heads]
    qk = [_dot_nt(q[h].astype(BF16), k16[h]) for h in heads]
    n = _unit_lower_inverses_minus_eye([jnp.where(strict, kk[h] * decay[h], 0.0) for h in heads], c)
    sol = [rhs[h] + _dot(n[h].astype(BF16), rhs[h].astype(BF16)) for h in heads]
    st = [s1_ref[0, h] for h in heads]
    st16 = [x.astype(BF16) for x in st]
    ws = [_dot(sol[h][:, DV_B:].astype(BF16), st16[h]) for h in heads]
    qs = [_dot((q[h] * egc_cols[:, h:h + 1]).astype(BF16), st16[h]) for h in heads]
    v_new16 = [(sol[h][:, :DV_B] - ws[h]).astype(BF16) for h in heads]
    av = [_dot((qk[h] * decay[h]).astype(BF16), v_new16[h]) for h in heads]
    upd = []
    for h in heads:
        kd = k[h] * jnp.exp(gc_cols[c - 1:c, h:h + 1] - gc_cols[:, h:h + 1])
        upd.append(_dot_tn(kd.astype(BF16), v_new16[h]))
    for h in heads:
        s1_ref[0, h] = jnp.exp(gc_cols[c - 1:c, h:h + 1]) * st[h] + upd[h]
        zz = z_ref[:, h * DV_B:(h + 1) * DV_B]
        yo = _rms(qs[h] + av[h], nw_ref[...]) * (zz * jax.nn.sigmoid(zz))
        mix_ref[:, h * DV_B:(h + 1) * DV_B] = yo.astype(mix_ref.dtype)


def _gdn(p, s0, conv0, conv_w, a_log, dt_bias, norm_w, *, batch, c, nc, row0, out_dtype):
    rb = row0 // c

    def rows(width, off):
        return pl.BlockSpec((c, width), lambda b, i: (rb + b * nc + i, off // width))

    state = lambda shape: pl.BlockSpec((1,) + shape, lambda b, i: (b,) + (0,) * len(shape))
    small = lambda shape: pl.BlockSpec(shape, lambda b, i: (0,) * len(shape))
    return pl.pallas_call(
        functools.partial(_gdn_kernel, c=c),
        grid=(batch, nc),
        in_specs=[rows(GDN_QKV, OFF_BQKV), rows(GDN_V, OFF_BZ), rows(LANES, OFF_GATES),
                  state((HB, DK_B, DV_B)), state((CONV_W - 1, GDN_QKV)),
                  small((CONV_W, GDN_QKV)), small((1, HB)), small((1, HB)), small((1, DV_B))],
        out_specs=[pl.BlockSpec((c, GDN_V), lambda b, i: (b * nc + i, 0)),
                   state((HB, DK_B, DV_B)), state((CONV_W - 1, GDN_QKV))],
        out_shape=[jax.ShapeDtypeStruct((batch * nc * c, GDN_V), out_dtype),
                   jax.ShapeDtypeStruct((batch, HB, DK_B, DV_B), F32),
                   jax.ShapeDtypeStruct((batch, CONV_W - 1, GDN_QKV), F32)],
        scratch_shapes=[pltpu.VMEM((c + 8, GDN_QKV), F32)],
        compiler_params=_params(("parallel", "arbitrary")),
        name="gdn",
    )(p, p, p, s0, conv0, conv_w, a_log, dt_bias, norm_w)


def _rope(x, cos, sin, first_half):
    outs = []
    for j in range(x.shape[1] // LANES):
        xs = x[:, j * LANES:(j + 1) * LANES]
        rot = jnp.where(first_half, -pltpu.roll(xs, LANES - HD_C // 2, 1), pltpu.roll(xs, HD_C // 2, 1))
        outs.append(xs * cos + rot * sin)
    return outs[0] if len(outs) == 1 else jnp.concatenate(outs, axis=1)


def _sink_softmax_terms(s, valid, sink):
    s = jnp.where(valid, s * HD_C ** -0.5, -jnp.inf)
    mx = jnp.maximum(jnp.max(s, axis=1, keepdims=True), sink)
    e = jnp.exp(s - mx)
    return e, 1.0 / (jnp.sum(e, axis=1, keepdims=True) + jnp.exp(sink - mx))


def _dup_halves(x, lo):
    out = []
    for j in range(x.shape[1] // LANES):
        slab = x[:, j * LANES:(j + 1) * LANES]
        swapped = pltpu.roll(slab, HD_C, 1)
        out += [jnp.where(lo, slab, swapped), jnp.where(lo, swapped, slab)]
    return jnp.concatenate(out, axis=1)


def _swa_prompt_kernel(q_ref, k_ref, v_ref, cos_ref, sin_ref, sink_ref, o_ref, kk_ref, vk_ref, kb_ref, vb_ref):
    nblk = pl.program_id(1)
    w = WINDOW

    @pl.when(nblk == 0)
    def _():
        kb_ref[0:w, :] = jnp.zeros((w, kb_ref.shape[1]), BF16)
        vb_ref[0:w, :] = jnp.zeros((w, vb_ref.shape[1]), BF16)

    cos, sin = cos_ref[...], sin_ref[...]
    lane = lax.broadcasted_iota(jnp.int32, (w, LANES), 1)
    first_half = (lane & (HD_C - 1)) < HD_C // 2
    lo = lane < HD_C
    q = _rope(q_ref[...], cos, sin, first_half).astype(BF16)
    k = _rope(k_ref[...], cos, sin, first_half)
    v = v_ref[...]
    kk_ref[0] = k
    vk_ref[0] = v
    kb_ref[w:2 * w, :] = _dup_halves(k, lo).astype(BF16)
    vb_ref[w:2 * w, :] = _dup_halves(v, lo).astype(BF16)
    rows = GROUP * w
    qi = lax.broadcasted_iota(jnp.int32, (rows, 2 * w), 0) & (w - 1)
    kj = lax.broadcasted_iota(jnp.int32, (rows, 2 * w), 1)
    valid = (kj > qi) & (kj <= qi + w) & ((kj >= w) | (nblk > 0))
    zero = jnp.zeros((w, LANES), BF16)
    for g0 in range(0, HKV_C, SWA_STAGE_KV):
        gs = range(g0, g0 + SWA_STAGE_KV)
        sc, sinks = [], []
        for g in gs:
            slabs = [q[:, (GROUP // 2 * g + j) * LANES:(GROUP // 2 * g + j + 1) * LANES] for j in range(GROUP // 2)]
            q4 = jnp.concatenate([jnp.where(keep, s, zero) for s in slabs for keep in (lo, ~lo)], axis=0)
            sc.append(_dot_nt(q4, kb_ref[:, g * LANES:(g + 1) * LANES]))
            sinks.append(jnp.concatenate([jnp.broadcast_to(sink_ref[:, h:h + 1], (w, 1))
                                          for h in range(g * GROUP, (g + 1) * GROUP)], axis=0))
        terms = [_sink_softmax_terms(sc[i], valid, sinks[i]) for i in range(len(gs))]
        pv = [_dot(terms[i][0].astype(BF16), vb_ref[:, g * LANES:(g + 1) * LANES]) for i, g in enumerate(gs)]
        for i, g in enumerate(gs):
            o = pv[i] * terms[i][1]
            for j in range(GROUP // 2):
                pair = jnp.where(lo, o[2 * j * w:(2 * j + 1) * w, :], o[(2 * j + 1) * w:(2 * j + 2) * w, :])
                o_ref[:, (GROUP // 2 * g + j) * LANES:(GROUP // 2 * g + j + 1) * LANES] = pair.astype(o_ref.dtype)
    kb_ref[0:w, :] = kb_ref[w:2 * w, :]
    vb_ref[0:w, :] = vb_ref[w:2 * w, :]


def _swa_prompt(p, cos, sin, sinks, *, batch, seq):
    w = WINDOW
    nb = seq // w
    nq, nkv = HQ_C * HD_C, HKV_C * HD_C
    keep = pl.BlockSpec((1, w, nkv), lambda b, i: (b, 0, 0))
    tab = pl.BlockSpec((w, LANES), lambda b, i: (i, 0))
    return pl.pallas_call(
        _swa_prompt_kernel,
        grid=(batch, nb),
        in_specs=[pl.BlockSpec((w, nq), lambda b, i: (b * nb + i, 0)),
                  pl.BlockSpec((w, nkv), lambda b, i: (b * nb + i, nq // nkv)),
                  pl.BlockSpec((w, nkv), lambda b, i: (b * nb + i, nq // nkv + 1)),
                  tab, tab, pl.BlockSpec((1, HQ_C), lambda b, i: (0, 0))],
        out_specs=[pl.BlockSpec((w, nq), lambda b, i: (b * nb + i, 0)), keep, keep],
        out_shape=[jax.ShapeDtypeStruct((batch * seq, nq), BF16),
                   jax.ShapeDtypeStruct((batch, w, nkv), F32),
                   jax.ShapeDtypeStruct((batch, w, nkv), F32)],
        scratch_shapes=[pltpu.VMEM((2 * w, HKV_C * LANES), BF16), pltpu.VMEM((2 * w, HKV_C * LANES), BF16)],
        compiler_params=_params(("parallel", "arbitrary")),
        name="swa_prompt",
    )(p, p, p, cos, sin, sinks)


def _swa_sample_kernel(q_ref, k_ref, v_ref, kb_ref, vb_ref, cos_ref, sin_ref, sink_ref,
                       o_ref, kk_ref, vk_ref, kc_ref, vc_ref, *, seq):
    w = WINDOW
    span = 2 * w
    cos, sin = cos_ref[...], sin_ref[...]
    first_half = (lax.broadcasted_iota(jnp.int32, (seq, LANES), 1) & (HD_C - 1)) < HD_C // 2
    q = _rope(q_ref[...], cos, sin, first_half).astype(BF16)
    kc_ref[0:w, :] = kb_ref[0]
    vc_ref[0:w, :] = vb_ref[0]
    kc_ref[w:w + seq, :] = _rope(k_ref[...], cos, sin, first_half)
    vc_ref[w:w + seq, :] = v_ref[...]
    kc_ref[w + seq:span, :] = jnp.zeros((span - w - seq, kc_ref.shape[1]), F32)
    vc_ref[w + seq:span, :] = jnp.zeros((span - w - seq, vc_ref.shape[1]), F32)
    kk_ref[0] = kc_ref[seq:seq + w, :]
    vk_ref[0] = vc_ref[seq:seq + w, :]
    kcat, vcat = kc_ref[...].astype(BF16), vc_ref[...].astype(BF16)
    rows = GROUP * seq
    qi = lax.broadcasted_iota(jnp.int32, (rows, span), 0) & (seq - 1)
    kj = lax.broadcasted_iota(jnp.int32, (rows, span), 1)
    valid = (kj > qi) & (kj <= qi + w) & (kj + (PAST_LEN - w) >= 0)
    groups = range(HKV_C)
    heads = [range(g * GROUP, (g + 1) * GROUP) for g in groups]
    sc = [_dot_nt(jnp.concatenate([q[:, h * HD_C:(h + 1) * HD_C] for h in heads[g]], axis=0),
                  kcat[:, g * HD_C:(g + 1) * HD_C]) for g in groups]
    terms = []
    for g in groups:
        sink = jnp.concatenate([jnp.broadcast_to(sink_ref[:, h:h + 1], (seq, 1)) for h in heads[g]], axis=0)
        terms.append(_sink_softmax_terms(sc[g], valid, sink))
    pv = [_dot(terms[g][0].astype(BF16), vcat[:, g * HD_C:(g + 1) * HD_C]) for g in groups]
    for g in groups:
        o = pv[g] * terms[g][1]
        for hh, h in enumerate(heads[g]):
            o_ref[:, h * HD_C:(h + 1) * HD_C] = o[hh * seq:(hh + 1) * seq, :]


def _swa_sample(p, k_buf, v_buf, cos, sin, sinks, *, batch, seq, row0):
    w = WINDOW
    nq, nkv = HQ_C * HD_C, HKV_C * HD_C
    rb = row0 // seq
    keep = pl.BlockSpec((1, w, nkv), lambda b: (b, 0, 0))
    tab = pl.BlockSpec((seq, LANES), lambda b: (0, 0))
    return pl.pallas_call(
        functools.partial(_swa_sample_kernel, seq=seq),
        grid=(batch,),
        in_specs=[pl.BlockSpec((seq, nq), lambda b: (rb + b, 0)),
                  pl.BlockSpec((seq, nkv), lambda b: (rb + b, nq // nkv)),
                  pl.BlockSpec((seq, nkv), lambda b: (rb + b, nq // nkv + 1)),
                  keep, keep, tab, tab, pl.BlockSpec((1, HQ_C), lambda b: (0, 0))],
        out_specs=[pl.BlockSpec((seq, nq), lambda b: (b, 0)), keep, keep],
        out_shape=[jax.ShapeDtypeStruct((batch * seq, nq), F32),
                   jax.ShapeDtypeStruct((batch, w, nkv), F32),
                   jax.ShapeDtypeStruct((batch, w, nkv), F32)],
        scratch_shapes=[pltpu.VMEM((2 * w, nkv), F32), pltpu.VMEM((2 * w, nkv), F32)],
        compiler_params=_params(("parallel",)),
        name="swa_sample",
    )(p, p, p, k_buf, v_buf, cos, sin, sinks)


def _rope_tables(pos):
    half = HD_C // 2
    inv = ROPE_THETA ** (-jnp.arange(half, dtype=F32) / half)
    ang = pos.astype(F32)[:, None] * inv[None, :]
    reps = LANES // half
    return jnp.tile(jnp.cos(ang), (1, reps)), jnp.tile(jnp.sin(ang), (1, reps))


def kernel(x_prompt, x_sample, state_mlstm_C, state_mlstm_n, state_mlstm_m, state_gdn_S, state_gdn_conv,
           cache_swa_k, cache_swa_v, norm_w, ffn_w_gate, ffn_w_up, ffn_w_down, ab_w_in, ab_w_out,
           mlstm_b_i, mlstm_b_f, mlstm_norm_w, gdn_conv_w, gdn_a_log, gdn_dt_bias, gdn_norm_w,
           c_w_qkv, c_w_out, c_sinks):
    bp, lp, d = x_prompt.shape
    bs, ls, _ = x_sample.shape
    mp, ms = bp * lp, bs * ls
    x = jnp.concatenate([x_prompt.reshape(mp, d), x_sample.reshape(ms, d)], axis=0)

    fpad = FFN_PAD - FFN_DIM
    wg = jnp.pad(ffn_w_gate, ((0, 0), (0, 0), (0, 0), (0, fpad))).astype(BF16)
    wu = jnp.pad(ffn_w_up, ((0, 0), (0, 0), (0, 0), (0, fpad))).astype(BF16)
    wd = jnp.pad(ffn_w_down, ((0, 0), (0, 0), (0, fpad), (0, 0))).astype(BF16)
    o_ai = OFF_BQKV
    o_bq = o_ai + 2 * HA
    o_bz = o_bq + GDN_QKV
    o_bb = o_bz + GDN_V
    w_gates = jnp.concatenate([ab_w_in[:, :, o_ai:o_bq], ab_w_in[:, :, o_bb:]], axis=-1)
    w_gates = jnp.pad(w_gates, ((0, 0), (0, 0), (0, IN_AB_PAD - OFF_GATES - w_gates.shape[-1])))
    w_ab = jnp.concatenate([ab_w_in[:, :, :o_ai], ab_w_in[:, :, o_bq:o_bb], w_gates], axis=-1).astype(BF16)
    w_ab_out = ab_w_out.astype(BF16)
    w_c = c_w_qkv.astype(BF16)
    w_c_out = c_w_out.astype(BF16)

    cos_p, sin_p = _rope_tables(jnp.arange(lp, dtype=jnp.int32))
    cos_s, sin_s = _rope_tables(PAST_LEN + jnp.arange(ls, dtype=jnp.int32))
    nkv = HKV_C * HD_C

    nw = norm_w.reshape(DEPTH * 6, 1, d)
    rec_p, rec_s, win_p, win_s = [], [], [], []
    for layer in range(DEPTH):
        j = layer // 2
        x = _ffn(x, nw, wg, wu, wd, layer, 0)
        if layer % 2 == 0:
            p = _proj_in(x, nw, layer * 6 + 2, w_ab, j)
            b_i, b_f = mlstm_b_i[j][None, :], mlstm_b_f[j][None, :]
            a_log, dt_b, gnw = gdn_a_log[j][None, :], gdn_dt_bias[j][None, :], gdn_norm_w[j][None, :]
            nc = lp // CHUNK
            ma_p, c1, n1, m1 = _mlstm(
                p, jnp.zeros((bp, HA, DK_A, DV_A), F32), jnp.zeros((bp, HA, DK_A), F32),
                jnp.zeros((bp, 1, HA), F32), b_i, b_f, mlstm_norm_w[j],
                batch=bp, c=CHUNK, nc=nc, row0=0, out_dtype=BF16)
            mb_p, s1, cv1 = _gdn(
                p, jnp.zeros((bp, HB, DK_B, DV_B), F32), jnp.zeros((bp, CONV_W - 1, GDN_QKV), F32),
                gdn_conv_w[j], a_log, dt_b, gnw, batch=bp, c=CHUNK, nc=nc, row0=0, out_dtype=BF16)
            rec_p.append((c1, n1, m1.reshape(bp, HA), s1, cv1))
            ma_s, c1, n1, m1 = _mlstm(
                p, state_mlstm_C[j], state_mlstm_n[j], state_mlstm_m[j][:, None, :], b_i, b_f, mlstm_norm_w[j],
                batch=bs, c=ls, nc=1, row0=mp, out_dtype=F32)
            mb_s, s1, cv1 = _gdn(
                p, state_gdn_S[j], state_gdn_conv[j], gdn_conv_w[j], a_log, dt_b, gnw,
                batch=bs, c=ls, nc=1, row0=mp, out_dtype=F32)
            rec_s.append((c1, n1, m1.reshape(bs, HA), s1, cv1))
            x = _proj_out([ma_p, mb_p], [ma_s.astype(BF16), mb_s.astype(BF16)], w_ab_out, j, x, nw, layer * 6 + 3)
        else:
            p = _proj_in(x, nw, layer * 6 + 2, w_c, j)
            sinks = c_sinks[j][None, :]
            o_p, kk, vk = _swa_prompt(p, cos_p, sin_p, sinks, batch=bp, seq=lp)
            win_p.append((kk.reshape(bp, WINDOW, HKV_C, HD_C), vk.reshape(bp, WINDOW, HKV_C, HD_C)))
            o_s, kk, vk = _swa_sample(p, cache_swa_k[j].reshape(bs, WINDOW, nkv),
                                      cache_swa_v[j].reshape(bs, WINDOW, nkv), cos_s, sin_s, sinks,
                                      batch=bs, seq=ls, row0=mp)
            win_s.append((kk.reshape(bs, WINDOW, HKV_C, HD_C), vk.reshape(bs, WINDOW, HKV_C, HD_C)))
            x = _proj_out([o_p], [o_s.astype(BF16)], w_c_out, j, x, nw, layer * 6 + 3)
        x = _ffn(x, nw, wg, wu, wd, layer, 1)

    stack = lambda sts, i: jnp.stack([st[i] for st in sts])
    y_prompt = x[:mp].reshape(bp, lp, d)
    y_sample = x[mp:].reshape(bs, ls, d)
    return ((y_prompt, y_sample)
            + tuple(stack(rec_p, i) for i in range(5)) + tuple(stack(win_p, i) for i in range(2))
            + tuple(stack(rec_s, i) for i in range(5)) + tuple(stack(win_s, i) for i in range(2)))
```

```python
import functools

import jax
import jax.numpy as jnp
from jax import lax
from jax.experimental import pallas as pl
from jax.experimental.pallas import tpu as pltpu

F32 = jnp.float32
BF16 = jnp.bfloat16

D_MODEL = 2048
DEPTH = 4
PAST_LEN = 16384
HA, DK_A, DV_A = 4, 128, 256
A_QK, A_V = HA * DK_A, HA * DV_A
GATE_CAP = 15.0
HB, DK_B, DV_B = 8, 128, 128
GDN_QKV, GDN_V = HB * (2 * DK_B + DV_B), HB * DV_B
CONV_W = 4
HQ_C, HKV_C, HD_C = 32, 8, 64
GROUP = HQ_C // HKV_C
WINDOW = 128
ROPE_THETA = 10000.0
FFN_DIM = 5504
EPS = 1e-6

LANES = 128
FFN_TM, FFN_TF = 768, 512
PROJ_TM, PROJ_TN = 768, 1536
CAST_TILE = 256
OUT_TM = 256
CHUNK = 128
SWA_STAGE_KV = 2
OFF_AQ, OFF_AK, OFF_AV, OFF_AO = 0, A_QK, 2 * A_QK, 2 * A_QK + A_V
OFF_BQKV = 2 * A_QK + 2 * A_V
OFF_BZ = OFF_BQKV + GDN_QKV
OFF_GATES = OFF_BZ + GDN_V
IN_AB_PAD = 7680
VMEM_LIMIT = 56 * 1024 * 1024


def _rms(x, w):
    return x * lax.rsqrt(jnp.mean(x * x, axis=-1, keepdims=True) + EPS) * w


def _dot(a, b):
    return jnp.dot(a, b, preferred_element_type=F32)


def _dot_nt(a, b):
    return lax.dot_general(a, b, (((1,), (1,)), ((), ())), preferred_element_type=F32)


def _dot_tn(a, b):
    return lax.dot_general(a, b, (((0,), (0,)), ((), ())), preferred_element_type=F32)


def _params(sem):
    return pltpu.CompilerParams(dimension_semantics=sem, vmem_limit_bytes=VMEM_LIMIT)


def _cast_pad_kernel(x_ref, o_ref):
    r, n = x_ref.shape
    o_ref[:r, :n] = x_ref[...].astype(BF16)
    if o_ref.shape[1] > n:
        o_ref[:, n:] = jnp.zeros((o_ref.shape[0], o_ref.shape[1] - n), BF16)
    if o_ref.shape[0] > r:
        o_ref[r:, :] = jnp.zeros((o_ref.shape[0] - r, o_ref.shape[1]), BF16)


def _cast_pad(w, rows_out, cols_out, *, split_rows):
    l, r, n = w.shape
    if split_rows:
        assert rows_out == r and r % CAST_TILE == 0
        grid = (l, r // CAST_TILE)
        in_spec = pl.BlockSpec((None, CAST_TILE, n), lambda a, b: (a, b, 0))
        out_spec = pl.BlockSpec((None, CAST_TILE, cols_out), lambda a, b: (a, b, 0))
    else:
        assert cols_out == n and n % CAST_TILE == 0
        grid = (l, n // CAST_TILE)
        in_spec = pl.BlockSpec((None, r, CAST_TILE), lambda a, b: (a, 0, b))
        out_spec = pl.BlockSpec((None, rows_out, CAST_TILE), lambda a, b: (a, 0, b))
    return pl.pallas_call(
        _cast_pad_kernel, grid=grid, in_specs=[in_spec], out_specs=out_spec,
        out_shape=jax.ShapeDtypeStruct((l, rows_out, cols_out), BF16),
        compiler_params=_params(("parallel", "parallel")), name="cast_pad",
    )(w)


def _perm_ab_kernel(x_ref, o_ref):
    x = x_ref[...]
    o_ai = OFF_BQKV
    o_bq = o_ai + 2 * HA
    o_bb = o_bq + GDN_QKV + GDN_V
    o_ref[:, :OFF_BQKV] = x[:, :o_ai].astype(BF16)
    o_ref[:, OFF_BQKV:OFF_GATES] = x[:, o_bq:o_bb].astype(BF16)
    n_gates = 2 * HA + 2 * HB
    o_ref[:, OFF_GATES:] = jnp.zeros((x.shape[0], IN_AB_PAD - OFF_GATES), BF16)
    o_ref[:, OFF_GATES:OFF_GATES + 2 * HA] = x[:, o_ai:o_bq].astype(BF16)
    o_ref[:, OFF_GATES + 2 * HA:OFF_GATES + n_gates] = x[:, o_bb:].astype(BF16)


def _perm_ab(w):
    l, r, n = w.shape
    return pl.pallas_call(
        _perm_ab_kernel, grid=(l, r // CAST_TILE),
        in_specs=[pl.BlockSpec((None, CAST_TILE, n), lambda a, b: (a, b, 0))],
        out_specs=pl.BlockSpec((None, CAST_TILE, IN_AB_PAD), lambda a, b: (a, b, 0)),
        out_shape=jax.ShapeDtypeStruct((l, r, IN_AB_PAD), BF16),
        compiler_params=_params(("parallel", "parallel")), name="perm_ab",
    )(w)


def _ffn_kernel(*refs, nf_main, n_tail, tail_rows):
    x_ref, nwa_ref, nwb_ref, wg_ref, wu_ref, wd_ref = refs[:6]
    wgt, wut, wdt = (refs[6 + k * n_tail:6 + (k + 1) * n_tail] for k in range(3))
    rest = refs[6 + 3 * n_tail:]
    o_ref, h_ref = rest[0], rest[-1]
    f = pl.program_id(1)
    last = nf_main if n_tail else nf_main - 1

    def partial_out(wg, wu, wd):
        h = h_ref[...]
        g = _dot(h, wg)
        u = _dot(h, wu)
        return _dot((g * jax.nn.sigmoid(g) * u).astype(BF16), wd)

    def finish(acc):
        o_ref[...] = x_ref[...] + 0.5 * _rms(acc, nwb_ref[...])
        if tail_rows:
            @pl.when(pl.program_id(0) == pl.num_programs(0) - 1)
            def _():
                rest[1][...] = o_ref[o_ref.shape[0] - tail_rows:, :]

    @pl.when(f == 0)
    def _():
        h_ref[...] = _rms(x_ref[...], nwa_ref[...]).astype(BF16)
        o_ref[...] = partial_out(wg_ref[...], wu_ref[...], wd_ref[...])

    @pl.when((f > 0) & (f < last))
    def _():
        o_ref[...] += partial_out(wg_ref[...], wu_ref[...], wd_ref[...])

    @pl.when(f == last)
    def _():
        if n_tail:
            cat = lambda rs, axis: jnp.concatenate([r[...] for r in rs], axis=axis)
            finish(o_ref[...] + partial_out(cat(wgt, 1), cat(wut, 1), cat(wdt, 0)))
        else:
            finish(o_ref[...] + partial_out(wg_ref[...], wu_ref[...], wd_ref[...]))


def _ffn(x, nw, wg, wu, wd, layer, half, head_rows=None):
    m, d = x.shape
    fdim = wg.shape[-1]
    nf_main, n_tail = fdim // FFN_TF, (fdim % FFN_TF) // LANES
    assert nf_main >= 2 and fdim == nf_main * FFN_TF + n_tail * LANES
    pre = layer * 6 + 4 * half
    wi = layer * 2 + half
    row = pl.BlockSpec((FFN_TM, d), lambda i, f: (i, 0))
    vec = lambda r: pl.BlockSpec((None, 1, d), lambda i, f: (r, 0, 0))
    main = lambda f: jnp.minimum(f, nf_main - 1)
    wcol = pl.BlockSpec((None, d, FFN_TF), lambda i, f: (wi, 0, main(f)))
    wrow = pl.BlockSpec((None, FFN_TF, d), lambda i, f: (wi, main(f), 0))
    t0 = nf_main * FFN_TF // LANES
    tcol = [pl.BlockSpec((None, d, LANES), functools.partial(lambda i, f, t: (wi, 0, t0 + t), t=t)) for t in range(n_tail)]
    trow = [pl.BlockSpec((None, LANES, d), functools.partial(lambda i, f, t: (wi, t0 + t, 0), t=t)) for t in range(n_tail)]
    tail_rows = 0 if head_rows is None else m - head_rows
    assert tail_rows <= FFN_TM
    out_specs, out_shape = row, jax.ShapeDtypeStruct((m, d), F32)
    if tail_rows:
        out_specs = [row, pl.BlockSpec((tail_rows, d), lambda i, f: (0, 0))]
        out_shape = [jax.ShapeDtypeStruct((head_rows, d), F32), jax.ShapeDtypeStruct((tail_rows, d), F32)]
    return pl.pallas_call(
        functools.partial(_ffn_kernel, nf_main=nf_main, n_tail=n_tail, tail_rows=tail_rows),
        grid=(m // FFN_TM, nf_main + (1 if n_tail else 0)),
        in_specs=[row, vec(pre), vec(pre + 1), wcol, wcol, wrow] + tcol + tcol + trow,
        out_specs=out_specs,
        out_shape=out_shape,
        scratch_shapes=[pltpu.VMEM((FFN_TM, d), BF16)],
        compiler_params=_params(("parallel", "arbitrary")),
        name="ffn",
    )(x, nw, nw, wg, wu, wd, *([wg] * n_tail), *([wu] * n_tail), *([wd] * n_tail))


def _proj_in_kernel(x_ref, nw_ref, w_ref, o_ref, h_ref):
    @pl.when(pl.program_id(1) == 0)
    def _():
        h_ref[...] = _rms(x_ref[...], nw_ref[...]).astype(BF16)
        o_ref[...] = _dot(h_ref[...], w_ref[...])

    @pl.when(pl.program_id(1) != 0)
    def _():
        o_ref[...] = _dot(h_ref[...], w_ref[...])


def _proj_in(x, nw, nw_row, w, j):
    m, d = x.shape
    n = w.shape[-1]
    return pl.pallas_call(
        _proj_in_kernel,
        grid=(m // PROJ_TM, n // PROJ_TN),
        in_specs=[pl.BlockSpec((PROJ_TM, d), lambda i, k: (i, 0)),
                  pl.BlockSpec((None, 1, d), lambda i, k: (nw_row, 0, 0)),
                  pl.BlockSpec((None, d, PROJ_TN), lambda i, k: (j, 0, k))],
        out_specs=pl.BlockSpec((PROJ_TM, PROJ_TN), lambda i, k: (i, k)),
        out_shape=jax.ShapeDtypeStruct((m, n), F32),
        scratch_shapes=[pltpu.VMEM((PROJ_TM, d), BF16)],
        compiler_params=_params(("parallel", "arbitrary")),
        name="proj_in",
    )(x, nw, w)


def _proj_out_kernel(*refs, n_parts, prompt_tiles):
    p_refs, s_refs, w_refs = refs[:n_parts], refs[n_parts:2 * n_parts], refs[2 * n_parts:3 * n_parts]
    x_ref, nw_ref, o_ref = refs[3 * n_parts:]

    def emit(a_refs):
        y = _dot(a_refs[0][...], w_refs[0][...])
        for a_ref, w_ref in zip(a_refs[1:], w_refs[1:]):
            y += _dot(a_ref[...], w_ref[...])
        o_ref[...] = x_ref[...] + _rms(y, nw_ref[...])

    is_prompt = pl.program_id(0) < prompt_tiles
    pl.when(is_prompt)(lambda: emit(p_refs))
    pl.when(jnp.logical_not(is_prompt))(lambda: emit(s_refs))


def _proj_out(p_parts, s_parts, w, j, x, nw, nw_row):
    m, d = x.shape
    n_parts = len(p_parts)
    mp, ms = p_parts[0].shape[0], s_parts[0].shape[0]
    assert mp % OUT_TM == 0 and ms == OUT_TM and mp + ms == m
    prompt_tiles = mp // OUT_TM
    kw = [p.shape[1] for p in p_parts]
    assert len(set(kw)) == 1 and sum(kw) == w.shape[1]
    in_specs = [pl.BlockSpec((OUT_TM, k), lambda i: (jnp.minimum(i, prompt_tiles - 1), 0)) for k in kw]
    in_specs += [pl.BlockSpec((OUT_TM, k), lambda i: (0, 0)) for k in kw]
    in_specs += [pl.BlockSpec((None, k, d), functools.partial(lambda i, part: (j, part, 0), part=part))
                 for part, k in enumerate(kw)]
    in_specs += [pl.BlockSpec((OUT_TM, d), lambda i: (i, 0)), pl.BlockSpec((None, 1, d), lambda i: (nw_row, 0, 0))]
    return pl.pallas_call(
        functools.partial(_proj_out_kernel, n_parts=n_parts, prompt_tiles=prompt_tiles),
        grid=(m // OUT_TM,),
        in_specs=in_specs,
        out_specs=pl.BlockSpec((OUT_TM, d), lambda i: (i, 0)),
        out_shape=jax.ShapeDtypeStruct((m, d), F32),
        compiler_params=_params(("parallel",)),
        name="proj_out",
    )(*p_parts, *s_parts, *([w] * n_parts), x, nw)


def _iota2(c):
    return lax.broadcasted_iota(jnp.int32, (c, c), 0), lax.broadcasted_iota(jnp.int32, (c, c), 1)


def _split3(x):
    a = x.astype(BF16)
    r = x - a.astype(F32)
    b = r.astype(BF16)
    return a, b, (r - b.astype(F32)).astype(BF16)


def _mask_dot(mask01, x):
    a, b, d = _split3(x)
    return _dot(mask01, a) + _dot(mask01, b) + _dot(mask01, d)


def _mask_dot_t(x, mask01):
    a, b, d = _split3(x)
    return _dot_tn(a, mask01) + _dot_tn(b, mask01) + _dot_tn(d, mask01)


def _softcap(x, cap):
    return cap * jnp.tanh(x / cap)


def _mlstm_kernel(q_ref, k_ref, v_ref, og_ref, g_ref, c0_ref, n0_ref, m0_ref, bi_ref, bf_ref, nw_ref,
                  mix_ref, c1_ref, n1_ref, m1_ref, *, c):
    @pl.when(pl.program_id(1) == 0)
    def _():
        c1_ref[...] = c0_ref[...]
        n1_ref[...] = n0_ref[...]
        m1_ref[...] = m0_ref[...]

    row, col = _iota2(c)
    causal = col <= row
    eye01 = (row == col).astype(BF16)
    low01 = causal.astype(BF16)
    upp01 = (row <= col).astype(BF16)
    gates = g_ref[...]
    ig_cols = _softcap(gates[:, 0:HA] + bi_ref[...], GATE_CAP)
    lf_cols = jax.nn.log_sigmoid(_softcap(gates[:, HA:2 * HA] + bf_ref[...], GATE_CAP))
    b_cols = _mask_dot(low01, lf_cols)
    b_rows = _mask_dot_t(lf_cols, upp01)
    ig_rows = _mask_dot_t(ig_cols, eye01)
    heads = range(HA)
    q = [q_ref[:, h * DK_A:(h + 1) * DK_A] * DK_A ** -0.5 for h in heads]
    k = [k_ref[:, h * DK_A:(h + 1) * DK_A] for h in heads]
    q16 = [x.astype(BF16) for x in q]
    k16 = [x.astype(BF16) for x in k]
    v16 = [v_ref[:, h * DV_A:(h + 1) * DV_A].astype(BF16) for h in heads]
    cp = [c1_ref[0, h] for h in heads]
    npv = [n1_ref[0, h:h + 1, :] for h in heads]
    qk = [_dot_nt(q16[h], k16[h]) for h in heads]
    qc = [_dot(q16[h], cp[h].astype(BF16)) for h in heads]
    s, s16, kw, m, w_inter, decay = [], [], [], [], [], []
    for h in heads:
        b_col, ig_col = b_cols[:, h:h + 1], ig_cols[:, h:h + 1]
        dmat = jnp.where(causal, b_col - b_rows[h:h + 1, :] + ig_rows[h:h + 1, :], -jnp.inf)
        inter = b_col + m1_ref[0, :, h:h + 1]
        mh = jnp.maximum(inter, jnp.max(dmat, axis=1, keepdims=True))
        sh = qk[h] * jnp.exp(dmat - mh)
        m_new = mh[c - 1:c, :]
        kw.append(k[h] * jnp.exp(b_col[c - 1:c, :] - b_col + ig_col - m_new))
        decay.append(jnp.exp(inter[c - 1:c, :] - m_new))
        w_inter.append(jnp.exp(inter - mh))
        m.append(mh)
        s.append(sh)
        s16.append(sh.astype(BF16))
    sv = [_dot(s16[h], v16[h]) for h in heads]
    upd = [_dot_tn(kw[h].astype(BF16), v16[h]) for h in heads]
    for h in heads:
        num = w_inter[h] * qc[h] + sv[h]
        den = (w_inter[h] * jnp.sum(q[h] * npv[h], axis=1, keepdims=True)
               + jnp.sum(s[h], axis=1, keepdims=True))
        hh = num / jnp.maximum(jnp.abs(den), jnp.exp(-m[h]))
        c1_ref[0, h] = decay[h] * cp[h] + upd[h]
        n1_ref[0, h:h + 1, :] = decay[h] * npv[h] + jnp.sum(kw[h], axis=0, keepdims=True)
        m1_ref[0, :, h:h + 1] = m[h][c - 1:c, :]
        y = _rms(hh, nw_ref[h:h + 1, :]) * jax.nn.sigmoid(og_ref[:, h * DV_A:(h + 1) * DV_A])
        mix_ref[:, h * DV_A:(h + 1) * DV_A] = y.astype(mix_ref.dtype)


def _mlstm(p, c0, n0, m0, b_i, b_f, norm_w, *, batch, c, nc, row0, out_dtype):
    rb = row0 // c

    def rows(width, off):
        return pl.BlockSpec((c, width), lambda b, i: (rb + b * nc + i, off // width))

    state = lambda shape: pl.BlockSpec((1,) + shape, lambda b, i: (b,) + (0,) * len(shape))
    small = lambda shape: pl.BlockSpec(shape, lambda b, i: (0,) * len(shape))
    return pl.pallas_call(
        functools.partial(_mlstm_kernel, c=c),
        grid=(batch, nc),
        in_specs=[rows(A_QK, OFF_AQ), rows(A_QK, OFF_AK), rows(A_V, OFF_AV), rows(A_V, OFF_AO),
                  rows(LANES, OFF_GATES),
                  state((HA, DK_A, DV_A)), state((HA, DK_A)), state((1, HA)),
                  small((1, HA)), small((1, HA)), small((HA, DV_A))],
        out_specs=[pl.BlockSpec((c, A_V), lambda b, i: (b * nc + i, 0)),
                   state((HA, DK_A, DV_A)), state((HA, DK_A)), state((1, HA))],
        out_shape=[jax.ShapeDtypeStruct((batch * nc * c, A_V), out_dtype),
                   jax.ShapeDtypeStruct((batch, HA, DK_A, DV_A), F32),
                   jax.ShapeDtypeStruct((batch, HA, DK_A), F32),
                   jax.ShapeDtypeStruct((batch, 1, HA), F32)],
        compiler_params=_params(("parallel", "arbitrary")),
        name="mlstm",
    )(p, p, p, p, p, c0, n0, m0, b_i, b_f, norm_w)


def _unit_lower_inverses_minus_eye(mats, c):
    row, col = _iota2(c)
    nb = min(c, 16)
    sh = nb.bit_length() - 1
    diag_blk = (row >> sh) == (col >> sh)
    mm = lambda x, y: _dot(x.astype(BF16), y.astype(BF16))
    p = [jnp.where(diag_blk, a, 0.0) for a in mats]
    n = [-x for x in p]
    k = 1
    while 2 * k < nb:
        p = [mm(x, x) for x in p]
        np_ = [mm(x, y) for x, y in zip(n, p)]
        n = [x + y + z for x, y, z in zip(n, p, np_)]
        k *= 2
    s = nb
    while s < c:
        sh = s.bit_length() - 1
        off_blk = ((row >> (sh + 1)) == (col >> (sh + 1))) & ((row >> sh) != (col >> sh))
        off = [jnp.where(off_blk, a, 0.0) for a in mats]
        x = [o + mm(y, o) for o, y in zip(off, n)]
        xn = [mm(y, z) for y, z in zip(x, n)]
        n = [z - (y + w) for z, y, w in zip(n, x, xn)]
        s *= 2
    return n


def _gdn_kernel(x_ref, z_ref, g_ref, s0_ref, conv0_ref, cw_ref, alog_ref, dtb_ref, nw_ref,
                mix_ref, s1_ref, conv1_ref, xp_ref, *, c):
    @pl.when(pl.program_id(1) == 0)
    def _():
        s1_ref[...] = s0_ref[...]
        xp_ref[5:8, :] = conv0_ref[0]

    xp_ref[8:8 + c, :] = x_ref[...]
    y = xp_ref[5:5 + c, :] * cw_ref[0:1, :]
    for j in range(1, CONV_W):
        y = y + xp_ref[5 + j:5 + j + c, :] * cw_ref[j:j + 1, :]
    qkv = y * jax.nn.sigmoid(y)
    tail = xp_ref[5 + c:8 + c, :]
    conv1_ref[0] = tail
    xp_ref[5:8, :] = tail

    row, col = _iota2(c)
    incl = col <= row
    strict = col < row
    gates = g_ref[...]
    beta_cols = jax.nn.sigmoid(gates[:, 2 * HA:2 * HA + HB])
    g_cols = -jnp.exp(alog_ref[...]) * jax.nn.softplus(gates[:, 2 * HA + HB:2 * HA + 2 * HB] + dtb_ref[...])
    gc_cols = _mask_dot(incl.astype(BF16), g_cols)
    gc_rows = _mask_dot_t(g_cols, (row <= col).astype(BF16))
    egc_cols = jnp.exp(gc_cols)
    heads = range(HB)
    q, k, k16, kbeta, rhs, decay = [], [], [], [], [], []
    for h in heads:
        qh = qkv[:, h * DK_B:(h + 1) * DK_B]
        kh = qkv[:, HB * DK_B + h * DK_B:HB * DK_B + (h + 1) * DK_B]
        vh = qkv[:, 2 * HB * DK_B + h * DV_B:2 * HB * DK_B + (h + 1) * DV_B]
        qh = qh * lax.rsqrt(jnp.sum(qh * qh, axis=-1, keepdims=True) + EPS) * DK_B ** -0.5
        kh = kh * lax.rsqrt(jnp.sum(kh * kh, axis=-1, keepdims=True) + EPS)
        beta = beta_cols[:, h:h + 1]
        kb = kh * beta
        q.append(qh)
        k.append(kh)
        k16.append(kh.astype(BF16))
        kbeta.append(kb)
        rhs.append(jnp.concatenate([vh * beta, kb * egc_cols[:, h:h + 1]], axis=1))
        decay.append(jnp.where(incl, jnp.exp(jnp.where(incl, gc_cols[:, h:h + 1] - gc_rows[h:h + 1, :], 0.0)), 0.0))
    kk = [_dot_nt(kbeta[h].astype(BF16), k16[h]) for h in heads]
    qk = [_dot_nt(q[h].astype(BF16), k16[h]) for h in heads]
    n = _unit_lower_inverses_minus_eye([jnp.where(strict, kk[h] * decay[h], 0.0) for h in heads], c)
    sol = [rhs[h] + _dot(n[h].astype(BF16), rhs[h].astype(BF16)) for h in heads]
    st = [s1_ref[0, h] for h in heads]
    st16 = [x.astype(BF16) for x in st]
    ws = [_dot(sol[h][:, DV_B:].astype(BF16), st16[h]) for h in heads]
    qs = [_dot((q[h] * egc_cols[:, h:h + 1]).astype(BF16), st16[h]) for h in heads]
    v_new16 = [(sol[h][:, :DV_B] - ws[h]).astype(BF16) for h in heads]
    av = [_dot((qk[h] * decay[h]).astype(BF16), v_new16[h]) for h in heads]
    upd = []
    for h in heads:
        kd = k[h] * jnp.exp(gc_cols[c - 1:c, h:h + 1] - gc_cols[:, h:h + 1])
        upd.append(_dot_tn(kd.astype(BF16), v_new16[h]))
    for h in heads:
        s1_ref[0, h] = jnp.exp(gc_cols[c - 1:c, h:h + 1]) * st[h] + upd[h]
        zz = z_ref[:, h * DV_B:(h + 1) * DV_B]
        yo = _rms(qs[h] + av[h], nw_ref[...]) * (zz * jax.nn.sigmoid(zz))
        mix_ref[:, h * DV_B:(h + 1) * DV_B] = yo.astype(mix_ref.dtype)


def _gdn(p, s0, conv0, conv_w, a_log, dt_bias, norm_w, *, batch, c, nc, row0, out_dtype):
    rb = row0 // c

    def rows(width, off):
        return pl.BlockSpec((c, width), lambda b, i: (rb + b * nc + i, off // width))

    state = lambda shape: pl.BlockSpec((1,) + shape, lambda b, i: (b,) + (0,) * len(shape))
    small = lambda shape: pl.BlockSpec(shape, lambda b, i: (0,) * len(shape))
    return pl.pallas_call(
        functools.partial(_gdn_kernel, c=c),
        grid=(batch, nc),
        in_specs=[rows(GDN_QKV, OFF_BQKV), rows(GDN_V, OFF_BZ), rows(LANES, OFF_GATES),
                  state((HB, DK_B, DV_B)), state((CONV_W - 1, GDN_QKV)),
                  small((CONV_W, GDN_QKV)), small((1, HB)), small((1, HB)), small((1, DV_B))],
        out_specs=[pl.BlockSpec((c, GDN_V), lambda b, i: (b * nc + i, 0)),
                   state((HB, DK_B, DV_B)), state((CONV_W - 1, GDN_QKV))],
        out_shape=[jax.ShapeDtypeStruct((batch * nc * c, GDN_V), out_dtype),
                   jax.ShapeDtypeStruct((batch, HB, DK_B, DV_B), F32),
                   jax.ShapeDtypeStruct((batch, CONV_W - 1, GDN_QKV), F32)],
        scratch_shapes=[pltpu.VMEM((c + 8, GDN_QKV), F32)],
        compiler_params=_params(("parallel", "arbitrary")),
        name="gdn",
    )(p, p, p, s0, conv0, conv_w, a_log, dt_bias, norm_w)


def _rope(x, cos, sin, first_half):
    outs = []
    for j in range(x.shape[1] // LANES):
        xs = x[:, j * LANES:(j + 1) * LANES]
        rot = jnp.where(first_half, -pltpu.roll(xs, LANES - HD_C // 2, 1), pltpu.roll(xs, HD_C // 2, 1))
        outs.append(xs * cos + rot * sin)
    return outs[0] if len(outs) == 1 else jnp.concatenate(outs, axis=1)


def _sink_softmax_terms(s, valid, sink):
    s = jnp.where(valid, s * HD_C ** -0.5, -jnp.inf)
    mx = jnp.maximum(jnp.max(s, axis=1, keepdims=True), sink)
    e = jnp.exp(s - mx)
    return e, 1.0 / (jnp.sum(e, axis=1, keepdims=True) + jnp.exp(sink - mx))


def _dup_halves(x, lo):
    out = []
    for j in range(x.shape[1] // LANES):
        slab = x[:, j * LANES:(j + 1) * LANES]
        swapped = pltpu.roll(slab, HD_C, 1)
        out += [jnp.where(lo, slab, swapped), jnp.where(lo, swapped, slab)]
    return jnp.concatenate(out, axis=1)


def _swa_prompt_kernel(q_ref, k_ref, v_ref, cos_ref, sin_ref, sink_ref, o_ref, kk_ref, vk_ref, kb_ref, vb_ref):
    nblk = pl.program_id(1)
    w = WINDOW

    @pl.when(nblk == 0)
    def _():
        kb_ref[0:w, :] = jnp.zeros((w, kb_ref.shape[1]), BF16)
        vb_ref[0:w, :] = jnp.zeros((w, vb_ref.shape[1]), BF16)

    cos, sin = cos_ref[...], sin_ref[...]
    lane = lax.broadcasted_iota(jnp.int32, (w, LANES), 1)
    first_half = (lane & (HD_C - 1)) < HD_C // 2
    lo = lane < HD_C
    q = _rope(q_ref[...], cos, sin, first_half).astype(BF16)
    k = _rope(k_ref[...], cos, sin, first_half)
    v = v_ref[...]
    kk_ref[0] = k
    vk_ref[0] = v
    kb_ref[w:2 * w, :] = _dup_halves(k, lo).astype(BF16)
    vb_ref[w:2 * w, :] = _dup_halves(v, lo).astype(BF16)
    rows = GROUP * w
    qi = lax.broadcasted_iota(jnp.int32, (rows, 2 * w), 0) & (w - 1)
    kj = lax.broadcasted_iota(jnp.int32, (rows, 2 * w), 1)
    valid = (kj > qi) & (kj <= qi + w) & ((kj >= w) | (nblk > 0))
    zero = jnp.zeros((w, LANES), BF16)
    for g0 in range(0, HKV_C, SWA_STAGE_KV):
        gs = range(g0, g0 + SWA_STAGE_KV)
        sc, sinks = [], []
        for g in gs:
            slabs = [q[:, (GROUP // 2 * g + j) * LANES:(GROUP // 2 * g + j + 1) * LANES] for j in range(GROUP // 2)]
            q4 = jnp.concatenate([jnp.where(keep, s, zero) for s in slabs for keep in (lo, ~lo)], axis=0)
            sc.append(_dot_nt(q4, kb_ref[:, g * LANES:(g + 1) * LANES]))
            sinks.append(jnp.concatenate([jnp.broadcast_to(sink_ref[:, h:h + 1], (w, 1))
                                          for h in range(g * GROUP, (g + 1) * GROUP)], axis=0))
        terms = [_sink_softmax_terms(sc[i], valid, sinks[i]) for i in range(len(gs))]
        pv = [_dot(terms[i][0].astype(BF16), vb_ref[:, g * LANES:(g + 1) * LANES]) for i, g in enumerate(gs)]
        for i, g in enumerate(gs):
            o = pv[i] * terms[i][1]
            for j in range(GROUP // 2):
                pair = jnp.where(lo, o[2 * j * w:(2 * j + 1) * w, :], o[(2 * j + 1) * w:(2 * j + 2) * w, :])
                o_ref[:, (GROUP // 2 * g + j) * LANES:(GROUP // 2 * g + j + 1) * LANES] = pair.astype(o_ref.dtype)
    kb_ref[0:w, :] = kb_ref[w:2 * w, :]
    vb_ref[0:w, :] = vb_ref[w:2 * w, :]


def _swa_prompt(p, cos, sin, sinks, *, batch, seq):
    w = WINDOW
    nb = seq // w
    nq, nkv = HQ_C * HD_C, HKV_C * HD_C
    keep = pl.BlockSpec((1, w, nkv), lambda b, i: (b, 0, 0))
    tab = pl.BlockSpec((w, LANES), lambda b, i: (i, 0))
    return pl.pallas_call(
        _swa_prompt_kernel,
        grid=(batch, nb),
        in_specs=[pl.BlockSpec((w, nq), lambda b, i: (b * nb + i, 0)),
                  pl.BlockSpec((w, nkv), lambda b, i: (b * nb + i, nq // nkv)),
                  pl.BlockSpec((w, nkv), lambda b, i: (b * nb + i, nq // nkv + 1)),
                  tab, tab, pl.BlockSpec((1, HQ_C), lambda b, i: (0, 0))],
        out_specs=[pl.BlockSpec((w, nq), lambda b, i: (b * nb + i, 0)), keep, keep],
        out_shape=[jax.ShapeDtypeStruct((batch * seq, nq), BF16),
                   jax.ShapeDtypeStruct((batch, w, nkv), F32),
                   jax.ShapeDtypeStruct((batch, w, nkv), F32)],
        scratch_shapes=[pltpu.VMEM((2 * w, HKV_C * LANES), BF16), pltpu.VMEM((2 * w, HKV_C * LANES), BF16)],
        compiler_params=_params(("parallel", "arbitrary")),
        name="swa_prompt",
    )(p, p, p, cos, sin, sinks)


def _swa_sample_kernel(q_ref, k_ref, v_ref, kb_ref, vb_ref, cos_ref, sin_ref, sink_ref,
                       o_ref, kk_ref, vk_ref, kc_ref, vc_ref, *, seq):
    w = WINDOW
    span = 2 * w
    cos, sin = cos_ref[...], sin_ref[...]
    first_half = (lax.broadcasted_iota(jnp.int32, (seq, LANES), 1) & (HD_C - 1)) < HD_C // 2
    q = _rope(q_ref[...], cos, sin, first_half).astype(BF16)
    kc_ref[0:w, :] = kb_ref[0]
    vc_ref[0:w, :] = vb_ref[0]
    kc_ref[w:w + seq, :] = _rope(k_ref[...], cos, sin, first_half)
    vc_ref[w:w + seq, :] = v_ref[...]
    kc_ref[w + seq:span, :] = jnp.zeros((span - w - seq, kc_ref.shape[1]), F32)
    vc_ref[w + seq:span, :] = jnp.zeros((span - w - seq, vc_ref.shape[1]), F32)
    kk_ref[0] = kc_ref[seq:seq + w, :]
    vk_ref[0] = vc_ref[seq:seq + w, :]
    kcat, vcat = kc_ref[...].astype(BF16), vc_ref[...].astype(BF16)
    rows = GROUP * seq
    qi = lax.broadcasted_iota(jnp.int32, (rows, span), 0) & (seq - 1)
    kj = lax.broadcasted_iota(jnp.int32, (rows, span), 1)
    valid = (kj > qi) & (kj <= qi + w) & (kj + (PAST_LEN - w) >= 0)
    groups = range(HKV_C)
    heads = [range(g * GROUP, (g + 1) * GROUP) for g in groups]
    sc = [_dot_nt(jnp.concatenate([q[:, h * HD_C:(h + 1) * HD_C] for h in heads[g]], axis=0),
                  kcat[:, g * HD_C:(g + 1) * HD_C]) for g in groups]
    terms = []
    for g in groups:
        sink = jnp.concatenate([jnp.broadcast_to(sink_ref[:, h:h + 1], (seq, 1)) for h in heads[g]], axis=0)
        terms.append(_sink_softmax_terms(sc[g], valid, sink))
    pv = [_dot(terms[g][0].astype(BF16), vcat[:, g * HD_C:(g + 1) * HD_C]) for g in groups]
    for g in groups:
        o = pv[g] * terms[g][1]
        for hh, h in enumerate(heads[g]):
            o_ref[:, h * HD_C:(h + 1) * HD_C] = o[hh * seq:(hh + 1) * seq, :]


def _swa_sample(p, k_buf, v_buf, cos, sin, sinks, *, batch, seq, row0):
    w = WINDOW
    nq, nkv = HQ_C * HD_C, HKV_C * HD_C
    rb = row0 // seq
    keep = pl.BlockSpec((1, w, nkv), lambda b: (b, 0, 0))
    tab = pl.BlockSpec((seq, LANES), lambda b: (0, 0))
    return pl.pallas_call(
        functools.partial(_swa_sample_kernel, seq=seq),
        grid=(batch,),
        in_specs=[pl.BlockSpec((seq, nq), lambda b: (rb + b, 0)),
                  pl.BlockSpec((seq, nkv), lambda b: (rb + b, nq // nkv)),
                  pl.BlockSpec((seq, nkv), lambda b: (rb + b, nq // nkv + 1)),
                  keep, keep, tab, tab, pl.BlockSpec((1, HQ_C), lambda b: (0, 0))],
        out_specs=[pl.BlockSpec((seq, nq), lambda b: (b, 0)), keep, keep],
        out_shape=[jax.ShapeDtypeStruct((batch * seq, nq), F32),
                   jax.ShapeDtypeStruct((batch, w, nkv), F32),
                   jax.ShapeDtypeStruct((batch, w, nkv), F32)],
        scratch_shapes=[pltpu.VMEM((2 * w, nkv), F32), pltpu.VMEM((2 * w, nkv), F32)],
        compiler_params=_params(("parallel",)),
        name="swa_sample",
    )(p, p, p, k_buf, v_buf, cos, sin, sinks)


def _rope_tables(pos):
    half = HD_C // 2
    inv = ROPE_THETA ** (-jnp.arange(half, dtype=F32) / half)
    ang = pos.astype(F32)[:, None] * inv[None, :]
    reps = LANES // half
    return jnp.tile(jnp.cos(ang), (1, reps)), jnp.tile(jnp.sin(ang), (1, reps))


def kernel(x_prompt, x_sample, state_mlstm_C, state_mlstm_n, state_mlstm_m, state_gdn_S, state_gdn_conv,
           cache_swa_k, cache_swa_v, norm_w, ffn_w_gate, ffn_w_up, ffn_w_down, ab_w_in, ab_w_out,
           mlstm_b_i, mlstm_b_f, mlstm_norm_w, gdn_conv_w, gdn_a_log, gdn_dt_bias, gdn_norm_w,
           c_w_qkv, c_w_out, c_sinks):
    bp, lp, d = x_prompt.shape
    bs, ls, _ = x_sample.shape
    mp, ms = bp * lp, bs * ls
    x = jnp.concatenate([x_prompt.reshape(mp, d), x_sample.reshape(ms, d)], axis=0)

    wg = _cast_pad(ffn_w_gate.reshape(DEPTH * 2, d, FFN_DIM), d, FFN_DIM, split_rows=True)
    wu = _cast_pad(ffn_w_up.reshape(DEPTH * 2, d, FFN_DIM), d, FFN_DIM, split_rows=True)
    wd = _cast_pad(ffn_w_down.reshape(DEPTH * 2, FFN_DIM, d), FFN_DIM, d, split_rows=False)
    w_ab = _perm_ab(ab_w_in)
    w_ab_out = _cast_pad(ab_w_out, ab_w_out.shape[1], d, split_rows=True)
    w_c = _cast_pad(c_w_qkv, d, c_w_qkv.shape[2], split_rows=True)
    w_c_out = _cast_pad(c_w_out, c_w_out.shape[1], d, split_rows=True)

    cos_p, sin_p = _rope_tables(jnp.arange(lp, dtype=jnp.int32))
    cos_s, sin_s = _rope_tables(PAST_LEN + jnp.arange(ls, dtype=jnp.int32))
    nkv = HKV_C * HD_C

    nw = norm_w.reshape(DEPTH * 6, 1, d)
    rec_p, rec_s, win_p, win_s = [], [], [], []
    for layer in range(DEPTH):
        j = layer // 2
        x = _ffn(x, nw, wg, wu, wd, layer, 0)
        if layer % 2 == 0:
            p = _proj_in(x, nw, layer * 6 + 2, w_ab, j)
            b_i, b_f = mlstm_b_i[j][None, :], mlstm_b_f[j][None, :]
            a_log, dt_b, gnw = gdn_a_log[j][None, :], gdn_dt_bias[j][None, :], gdn_norm_w[j][None, :]
            nc = lp // CHUNK
            ma_p, c1, n1, m1 = _mlstm(
                p, jnp.zeros((bp, HA, DK_A, DV_A), F32), jnp.zeros((bp, HA, DK_A), F32),
                jnp.zeros((bp, 1, HA), F32), b_i, b_f, mlstm_norm_w[j],
                batch=bp, c=CHUNK, nc=nc, row0=0, out_dtype=BF16)
            mb_p, s1, cv1 = _gdn(
                p, jnp.zeros((bp, HB, DK_B, DV_B), F32), jnp.zeros((bp, CONV_W - 1, GDN_QKV), F32),
                gdn_conv_w[j], a_log, dt_b, gnw, batch=bp, c=CHUNK, nc=nc, row0=0, out_dtype=BF16)
            rec_p.append((c1, n1, m1.reshape(bp, HA), s1, cv1))
            ma_s, c1, n1, m1 = _mlstm(
                p, state_mlstm_C[j], state_mlstm_n[j], state_mlstm_m[j][:, None, :], b_i, b_f, mlstm_norm_w[j],
                batch=bs, c=ls, nc=1, row0=mp, out_dtype=F32)
            mb_s, s1, cv1 = _gdn(
                p, state_gdn_S[j], state_gdn_conv[j], gdn_conv_w[j], a_log, dt_b, gnw,
                batch=bs, c=ls, nc=1, row0=mp, out_dtype=F32)
            rec_s.append((c1, n1, m1.reshape(bs, HA), s1, cv1))
            x = _proj_out([ma_p, mb_p], [ma_s.astype(BF16), mb_s.astype(BF16)], w_ab_out, j, x, nw, layer * 6 + 3)
        else:
            p = _proj_in(x, nw, layer * 6 + 2, w_c, j)
            sinks = c_sinks[j][None, :]
            o_p, kk, vk = _swa_prompt(p, cos_p, sin_p, sinks, batch=bp, seq=lp)
            win_p.append((kk.reshape(bp, WINDOW, HKV_C, HD_C), vk.reshape(bp, WINDOW, HKV_C, HD_C)))
            o_s, kk, vk = _swa_sample(p, cache_swa_k[j].reshape(bs, WINDOW, nkv),
                                      cache_swa_v[j].reshape(bs, WINDOW, nkv), cos_s, sin_s, sinks,
                                      batch=bs, seq=ls, row0=mp)
            win_s.append((kk.reshape(bs, WINDOW, HKV_C, HD_C), vk.reshape(bs, WINDOW, HKV_C, HD_C)))
            x = _proj_out([o_p], [o_s.astype(BF16)], w_c_out, j, x, nw, layer * 6 + 3)
        x = _ffn(x, nw, wg, wu, wd, layer, 1, head_rows=mp if layer == DEPTH - 1 else None)

    stack = lambda sts, i: jnp.stack([st[i] for st in sts])
    y_prompt = x[0].reshape(bp, lp, d)
    y_sample = x[1].reshape(bs, ls, d)
    return ((y_prompt, y_sample)
            + tuple(stack(rec_p, i) for i in range(5)) + tuple(stack(win_p, i) for i in range(2))
            + tuple(stack(rec_s, i) for i in range(5)) + tuple(stack(win_s, i) for i in range(2)))
```

```python
import functools

import jax
import jax.numpy as jnp
from jax import lax
from jax.experimental import pallas as pl
from jax.experimental.pallas import tpu as pltpu

F32 = jnp.float32
BF16 = jnp.bfloat16

D_MODEL = 2048
DEPTH = 4
PAST_LEN = 16384
HA, DK_A, DV_A = 4, 128, 256
A_QK, A_V = HA * DK_A, HA * DV_A
GATE_CAP = 15.0
HB, DK_B, DV_B = 8, 128, 128
GDN_QKV, GDN_V = HB * (2 * DK_B + DV_B), HB * DV_B
CONV_W = 4
HQ_C, HKV_C, HD_C = 32, 8, 64
GROUP = HQ_C // HKV_C
WINDOW = 128
ROPE_THETA = 10000.0
FFN_DIM = 5504
EPS = 1e-6

LANES = 128
FFN_TM, FFN_TF = 528, 1024
PROJ_TM, PROJ_TN = 768, 1536
CAST_TILE = 256
OUT_TM = 256
CHUNK = 128
SAMPLE_NSEQ = 4
SWA_STAGE_KV = 2
OFF_AQ, OFF_AK, OFF_AV, OFF_AO = 0, A_QK, 2 * A_QK, 2 * A_QK + A_V
OFF_BQKV = 2 * A_QK + 2 * A_V
OFF_BZ = OFF_BQKV + GDN_QKV
OFF_GATES = OFF_BZ + GDN_V
IN_AB_PAD = 7680
VMEM_LIMIT = 56 * 1024 * 1024


def _rms(x, w):
    return x * lax.rsqrt(jnp.mean(x * x, axis=-1, keepdims=True) + EPS) * w


def _dot(a, b):
    return jnp.dot(a, b, preferred_element_type=F32)


def _dot_nt(a, b):
    return lax.dot_general(a, b, (((1,), (1,)), ((), ())), preferred_element_type=F32)


def _dot_tn(a, b):
    return lax.dot_general(a, b, (((0,), (0,)), ((), ())), preferred_element_type=F32)


def _params(sem):
    return pltpu.CompilerParams(dimension_semantics=sem, vmem_limit_bytes=VMEM_LIMIT)


def _cast_pad_kernel(x_ref, o_ref):
    r, n = x_ref.shape
    o_ref[:r, :n] = x_ref[...].astype(BF16)
    if o_ref.shape[1] > n:
        o_ref[:, n:] = jnp.zeros((o_ref.shape[0], o_ref.shape[1] - n), BF16)
    if o_ref.shape[0] > r:
        o_ref[r:, :] = jnp.zeros((o_ref.shape[0] - r, o_ref.shape[1]), BF16)


def _cast_pad(w, rows_out, cols_out, *, split_rows):
    l, r, n = w.shape
    if split_rows:
        assert rows_out == r and r % CAST_TILE == 0
        grid = (l, r // CAST_TILE)
        in_spec = pl.BlockSpec((None, CAST_TILE, n), lambda a, b: (a, b, 0))
        out_spec = pl.BlockSpec((None, CAST_TILE, cols_out), lambda a, b: (a, b, 0))
    else:
        assert cols_out == n and n % CAST_TILE == 0
        grid = (l, n // CAST_TILE)
        in_spec = pl.BlockSpec((None, r, CAST_TILE), lambda a, b: (a, 0, b))
        out_spec = pl.BlockSpec((None, rows_out, CAST_TILE), lambda a, b: (a, 0, b))
    return pl.pallas_call(
        _cast_pad_kernel, grid=grid, in_specs=[in_spec], out_specs=out_spec,
        out_shape=jax.ShapeDtypeStruct((l, rows_out, cols_out), BF16),
        compiler_params=_params(("parallel", "parallel")), name="cast_pad",
    )(w)


def _perm_ab_kernel(x_ref, o_ref):
    x = x_ref[...]
    o_ai = OFF_BQKV
    o_bq = o_ai + 2 * HA
    o_bb = o_bq + GDN_QKV + GDN_V
    o_ref[:, :OFF_BQKV] = x[:, :o_ai].astype(BF16)
    o_ref[:, OFF_BQKV:OFF_GATES] = x[:, o_bq:o_bb].astype(BF16)
    n_gates = 2 * HA + 2 * HB
    o_ref[:, OFF_GATES:] = jnp.zeros((x.shape[0], IN_AB_PAD - OFF_GATES), BF16)
    o_ref[:, OFF_GATES:OFF_GATES + 2 * HA] = x[:, o_ai:o_bq].astype(BF16)
    o_ref[:, OFF_GATES + 2 * HA:OFF_GATES + n_gates] = x[:, o_bb:].astype(BF16)


def _perm_ab(w):
    l, r, n = w.shape
    return pl.pallas_call(
        _perm_ab_kernel, grid=(l, r // CAST_TILE),
        in_specs=[pl.BlockSpec((None, CAST_TILE, n), lambda a, b: (a, b, 0))],
        out_specs=pl.BlockSpec((None, CAST_TILE, IN_AB_PAD), lambda a, b: (a, b, 0)),
        out_shape=jax.ShapeDtypeStruct((l, r, IN_AB_PAD), BF16),
        compiler_params=_params(("parallel", "parallel")), name="perm_ab",
    )(w)


def _ffn_kernel(*refs, nf_main, n_tail, tail_rows):
    x_ref, nwa_ref, nwb_ref, wg_ref, wu_ref, wd_ref = refs[:6]
    wgt, wut, wdt = (refs[6 + k * n_tail:6 + (k + 1) * n_tail] for k in range(3))
    rest = refs[6 + 3 * n_tail:]
    o_ref, h_ref = rest[0], rest[-1]
    f = pl.program_id(1)
    last = nf_main if n_tail else nf_main - 1

    def partial_out(wg, wu, wd):
        h = h_ref[...]
        g = _dot(h, wg)
        u = _dot(h, wu)
        return _dot((g * jax.nn.sigmoid(g) * u).astype(BF16), wd)

    def finish(acc):
        o_ref[...] = x_ref[...] + 0.5 * _rms(acc, nwb_ref[...])
        if tail_rows:
            @pl.when(pl.program_id(0) == pl.num_programs(0) - 1)
            def _():
                rest[1][...] = o_ref[o_ref.shape[0] - tail_rows:, :]

    @pl.when(f == 0)
    def _():
        h_ref[...] = _rms(x_ref[...], nwa_ref[...]).astype(BF16)
        o_ref[...] = partial_out(wg_ref[...], wu_ref[...], wd_ref[...])

    @pl.when((f > 0) & (f < last))
    def _():
        o_ref[...] += partial_out(wg_ref[...], wu_ref[...], wd_ref[...])

    @pl.when(f == last)
    def _():
        if n_tail:
            cat = lambda rs, axis: jnp.concatenate([r[...] for r in rs], axis=axis)
            finish(o_ref[...] + partial_out(cat(wgt, 1), cat(wut, 1), cat(wdt, 0)))
        else:
            finish(o_ref[...] + partial_out(wg_ref[...], wu_ref[...], wd_ref[...]))


def _ffn(x, nw, wg, wu, wd, layer, half, head_rows=None):
    m, d = x.shape
    fdim = wg.shape[-1]
    nf_main, n_tail = fdim // FFN_TF, (fdim % FFN_TF) // LANES
    assert nf_main >= 2 and fdim == nf_main * FFN_TF + n_tail * LANES
    pre = layer * 6 + 4 * half
    wi = layer * 2 + half
    row = pl.BlockSpec((FFN_TM, d), lambda i, f: (i, 0))
    vec = lambda r: pl.BlockSpec((None, 1, d), lambda i, f: (r, 0, 0))
    main = lambda f: jnp.minimum(f, nf_main - 1)
    wcol = pl.BlockSpec((None, d, FFN_TF), lambda i, f: (wi, 0, main(f)))
    wrow = pl.BlockSpec((None, FFN_TF, d), lambda i, f: (wi, main(f), 0))
    t0 = nf_main * FFN_TF // LANES
    tcol = [pl.BlockSpec((None, d, LANES), functools.partial(lambda i, f, t: (wi, 0, t0 + t), t=t)) for t in range(n_tail)]
    trow = [pl.BlockSpec((None, LANES, d), functools.partial(lambda i, f, t: (wi, t0 + t, 0), t=t)) for t in range(n_tail)]
    tail_rows = 0 if head_rows is None else m - head_rows
    assert tail_rows <= FFN_TM
    out_specs, out_shape = row, jax.ShapeDtypeStruct((m, d), F32)
    if tail_rows:
        out_specs = [row, pl.BlockSpec((tail_rows, d), lambda i, f: (0, 0))]
        out_shape = [jax.ShapeDtypeStruct((head_rows, d), F32), jax.ShapeDtypeStruct((tail_rows, d), F32)]
    return pl.pallas_call(
        functools.partial(_ffn_kernel, nf_main=nf_main, n_tail=n_tail, tail_rows=tail_rows),
        grid=(m // FFN_TM, nf_main + (1 if n_tail else 0)),
        in_specs=[row, vec(pre), vec(pre + 1), wcol, wcol, wrow] + tcol + tcol + trow,
        out_specs=out_specs,
        out_shape=out_shape,
        scratch_shapes=[pltpu.VMEM((FFN_TM, d), BF16)],
        compiler_params=_params(("parallel", "arbitrary")),
        name="ffn",
    )(x, nw, nw, wg, wu, wd, *([wg] * n_tail), *([wu] * n_tail), *([wd] * n_tail))


def _proj_in_kernel(x_ref, nw_ref, w_ref, o_ref, h_ref):
    @pl.when(pl.program_id(1) == 0)
    def _():
        h_ref[...] = _rms(x_ref[...], nw_ref[...]).astype(BF16)
        o_ref[...] = _dot(h_ref[...], w_ref[...])

    @pl.when(pl.program_id(1) != 0)
    def _():
        o_ref[...] = _dot(h_ref[...], w_ref[...])


def _proj_in(x, nw, nw_row, w, j):
    m, d = x.shape
    n = w.shape[-1]
    return pl.pallas_call(
        _proj_in_kernel,
        grid=(m // PROJ_TM, n // PROJ_TN),
        in_specs=[pl.BlockSpec((PROJ_TM, d), lambda i, k: (i, 0)),
                  pl.BlockSpec((None, 1, d), lambda i, k: (nw_row, 0, 0)),
                  pl.BlockSpec((None, d, PROJ_TN), lambda i, k: (j, 0, k))],
        out_specs=pl.BlockSpec((PROJ_TM, PROJ_TN), lambda i, k: (i, k)),
        out_shape=jax.ShapeDtypeStruct((m, n), F32),
        scratch_shapes=[pltpu.VMEM((PROJ_TM, d), BF16)],
        compiler_params=_params(("parallel", "arbitrary")),
        name="proj_in",
    )(x, nw, w)


def _proj_out_kernel(*refs, n_parts, prompt_tiles):
    p_refs, s_refs, w_refs = refs[:n_parts], refs[n_parts:2 * n_parts], refs[2 * n_parts:3 * n_parts]
    x_ref, nw_ref, o_ref = refs[3 * n_parts:]

    def emit(a_refs):
        y = _dot(a_refs[0][...], w_refs[0][...])
        for a_ref, w_ref in zip(a_refs[1:], w_refs[1:]):
            y += _dot(a_ref[...], w_ref[...])
        o_ref[...] = x_ref[...] + _rms(y, nw_ref[...])

    is_prompt = pl.program_id(0) < prompt_tiles
    pl.when(is_prompt)(lambda: emit(p_refs))
    pl.when(jnp.logical_not(is_prompt))(lambda: emit(s_refs))


def _proj_out(p_parts, s_parts, w, j, x, nw, nw_row):
    m, d = x.shape
    n_parts = len(p_parts)
    mp, ms = p_parts[0].shape[0], s_parts[0].shape[0]
    assert mp % OUT_TM == 0 and ms == OUT_TM and mp + ms == m
    prompt_tiles = mp // OUT_TM
    kw = [p.shape[1] for p in p_parts]
    assert len(set(kw)) == 1 and sum(kw) == w.shape[1]
    in_specs = [pl.BlockSpec((OUT_TM, k), lambda i: (jnp.minimum(i, prompt_tiles - 1), 0)) for k in kw]
    in_specs += [pl.BlockSpec((OUT_TM, k), lambda i: (0, 0)) for k in kw]
    in_specs += [pl.BlockSpec((None, k, d), functools.partial(lambda i, part: (j, part, 0), part=part))
                 for part, k in enumerate(kw)]
    in_specs += [pl.BlockSpec((OUT_TM, d), lambda i: (i, 0)), pl.BlockSpec((None, 1, d), lambda i: (nw_row, 0, 0))]
    return pl.pallas_call(
        functools.partial(_proj_out_kernel, n_parts=n_parts, prompt_tiles=prompt_tiles),
        grid=(m // OUT_TM,),
        in_specs=in_specs,
        out_specs=pl.BlockSpec((OUT_TM, d), lambda i: (i, 0)),
        out_shape=jax.ShapeDtypeStruct((m, d), F32),
        compiler_params=_params(("parallel",)),
        name="proj_out",
    )(*p_parts, *s_parts, *([w] * n_parts), x, nw)


def _iota2(c):
    return lax.broadcasted_iota(jnp.int32, (c, c), 0), lax.broadcasted_iota(jnp.int32, (c, c), 1)


def _split3(x):
    a = x.astype(BF16)
    r = x - a.astype(F32)
    b = r.astype(BF16)
    return a, b, (r - b.astype(F32)).astype(BF16)


def _mask_dot(mask01, x):
    a, b, d = _split3(x)
    return _dot(mask01, a) + _dot(mask01, b) + _dot(mask01, d)


def _mask_dot_t(x, mask01):
    a, b, d = _split3(x)
    return _dot_tn(a, mask01) + _dot_tn(b, mask01) + _dot_tn(d, mask01)


def _softcap(x, cap):
    return cap * jnp.tanh(x / cap)


def _mlstm_kernel(q_ref, k_ref, v_ref, og_ref, g_ref, c0_ref, n0_ref, m0_ref, bi_ref, bf_ref, nw_ref,
                  mix_ref, c1_ref, n1_ref, m1_ref, *, c, nseq):
    @pl.when(pl.program_id(1) == 0)
    def _():
        c1_ref[...] = c0_ref[...]
        n1_ref[...] = n0_ref[...]
        m1_ref[...] = m0_ref[...]

    row, col = _iota2(c)
    causal = col <= row
    eye01 = (row == col).astype(BF16)
    low01 = causal.astype(BF16)
    upp01 = (row <= col).astype(BF16)
    ig_cols, b_cols, b_rows, ig_rows = [], [], [], []
    for s in range(nseq):
        gates = g_ref[s * c:(s + 1) * c, :]
        ig = _softcap(gates[:, 0:HA] + bi_ref[...], GATE_CAP)
        lf = jax.nn.log_sigmoid(_softcap(gates[:, HA:2 * HA] + bf_ref[...], GATE_CAP))
        ig_cols.append(ig)
        b_cols.append(_mask_dot(low01, lf))
        b_rows.append(_mask_dot_t(lf, upp01))
        ig_rows.append(_mask_dot_t(ig, eye01))
    units = [(s, h) for s in range(nseq) for h in range(HA)]
    idx = range(len(units))
    rows = lambda s: slice(s * c, (s + 1) * c)
    q = [q_ref[rows(s), h * DK_A:(h + 1) * DK_A] * DK_A ** -0.5 for s, h in units]
    k = [k_ref[rows(s), h * DK_A:(h + 1) * DK_A] for s, h in units]
    q16 = [x.astype(BF16) for x in q]
    k16 = [x.astype(BF16) for x in k]
    v16 = [v_ref[rows(s), h * DV_A:(h + 1) * DV_A].astype(BF16) for s, h in units]
    cp = [c1_ref[s, h] for s, h in units]
    npv = [n1_ref[s, h:h + 1, :] for s, h in units]
    qk = [_dot_nt(q16[u], k16[u]) for u in idx]
    qc = [_dot(q16[u], cp[u].astype(BF16)) for u in idx]
    sc, sc16, kw, m, w_inter, decay = [], [], [], [], [], []
    for u, (s, h) in enumerate(units):
        b_col, ig_col = b_cols[s][:, h:h + 1], ig_cols[s][:, h:h + 1]
        dmat = jnp.where(causal, b_col - b_rows[s][h:h + 1, :] + ig_rows[s][h:h + 1, :], -jnp.inf)
        inter = b_col + m1_ref[s, :, h:h + 1]
        mh = jnp.maximum(inter, jnp.max(dmat, axis=1, keepdims=True))
        su = qk[u] * jnp.exp(dmat - mh)
        m_new = mh[c - 1:c, :]
        kw.append(k[u] * jnp.exp(b_col[c - 1:c, :] - b_col + ig_col - m_new))
        decay.append(jnp.exp(inter[c - 1:c, :] - m_new))
        w_inter.append(jnp.exp(inter - mh))
        m.append(mh)
        sc.append(su)
        sc16.append(su.astype(BF16))
    sv = [_dot(sc16[u], v16[u]) for u in idx]
    upd = [_dot_tn(kw[u].astype(BF16), v16[u]) for u in idx]
    for u, (s, h) in enumerate(units):
        num = w_inter[u] * qc[u] + sv[u]
        den = (w_inter[u] * jnp.sum(q[u] * npv[u], axis=1, keepdims=True)
               + jnp.sum(sc[u], axis=1, keepdims=True))
        hh = num / jnp.maximum(jnp.abs(den), jnp.exp(-m[u]))
        c1_ref[s, h] = decay[u] * cp[u] + upd[u]
        n1_ref[s, h:h + 1, :] = decay[u] * npv[u] + jnp.sum(kw[u], axis=0, keepdims=True)
        m1_ref[s, :, h:h + 1] = m[u][c - 1:c, :]
        y = _rms(hh, nw_ref[h:h + 1, :]) * jax.nn.sigmoid(og_ref[rows(s), h * DV_A:(h + 1) * DV_A])
        mix_ref[rows(s), h * DV_A:(h + 1) * DV_A] = y.astype(mix_ref.dtype)


def _seq_specs(c, nc, nseq, row0, j):
    assert nseq == 1 or nc == 1
    rb = row0 // (nseq * c)
    assert rb * nseq * c == row0
    rows = lambda width, off: pl.BlockSpec((nseq * c, width), lambda b, i: (rb + b * nc + i, off // width))
    out_rows = lambda width: pl.BlockSpec((nseq * c, width), lambda b, i: (b * nc + i, 0))
    state_in = lambda shape: pl.BlockSpec((None, nseq) + shape, lambda b, i: (j, b) + (0,) * len(shape))
    state_out = lambda shape: pl.BlockSpec((nseq,) + shape, lambda b, i: (b,) + (0,) * len(shape))
    small = lambda shape: pl.BlockSpec((None,) + shape, lambda b, i: (j,) + (0,) * len(shape))
    return rows, out_rows, state_in, state_out, small


def _mlstm(p, c0, n0, m0, b_i, b_f, norm_w, j, *, batch, c, nc, nseq, row0, out_dtype):
    rows, out_rows, state_in, state_out, small = _seq_specs(c, nc, nseq, row0, j)
    return pl.pallas_call(
        functools.partial(_mlstm_kernel, c=c, nseq=nseq),
        grid=(batch // nseq, nc),
        in_specs=[rows(A_QK, OFF_AQ), rows(A_QK, OFF_AK), rows(A_V, OFF_AV), rows(A_V, OFF_AO),
                  rows(LANES, OFF_GATES),
                  state_in((HA, DK_A, DV_A)), state_in((HA, DK_A)), state_in((1, HA)),
                  small((1, HA)), small((1, HA)), small((HA, DV_A))],
        out_specs=[out_rows(A_V), state_out((HA, DK_A, DV_A)), state_out((HA, DK_A)), state_out((1, HA))],
        out_shape=[jax.ShapeDtypeStruct((batch * nc * c, A_V), out_dtype),
                   jax.ShapeDtypeStruct((batch, HA, DK_A, DV_A), F32),
                   jax.ShapeDtypeStruct((batch, HA, DK_A), F32),
                   jax.ShapeDtypeStruct((batch, 1, HA), F32)],
        compiler_params=_params(("parallel", "arbitrary")),
        name="mlstm",
    )(p, p, p, p, p, c0, n0, m0, b_i, b_f, norm_w)


def _unit_lower_inverses_minus_eye(mats, c):
    row, col = _iota2(c)
    nb = min(c, 16)
    sh = nb.bit_length() - 1
    diag_blk = (row >> sh) == (col >> sh)
    mm = lambda x, y: _dot(x.astype(BF16), y.astype(BF16))
    p = [jnp.where(diag_blk, a, 0.0) for a in mats]
    n = [-x for x in p]
    k = 1
    while 2 * k < nb:
        p = [mm(x, x) for x in p]
        np_ = [mm(x, y) for x, y in zip(n, p)]
        n = [x + y + z for x, y, z in zip(n, p, np_)]
        k *= 2
    s = nb
    while s < c:
        sh = s.bit_length() - 1
        off_blk = ((row >> (sh + 1)) == (col >> (sh + 1))) & ((row >> sh) != (col >> sh))
        off = [jnp.where(off_blk, a, 0.0) for a in mats]
        x = [o + mm(y, o) for o, y in zip(off, n)]
        xn = [mm(y, z) for y, z in zip(x, n)]
        n = [z - (y + w) for z, y, w in zip(n, x, xn)]
        s *= 2
    return n


def _gdn_kernel(x_ref, z_ref, g_ref, s0_ref, conv0_ref, cw_ref, alog_ref, dtb_ref, nw_ref,
                mix_ref, s1_ref, conv1_ref, xp_ref, *, c, nseq):
    @pl.when(pl.program_id(1) == 0)
    def _():
        s1_ref[...] = s0_ref[...]
        xp_ref[:, 5:8, :] = conv0_ref[...]

    row, col = _iota2(c)
    incl = col <= row
    strict = col < row
    low01, upp01 = incl.astype(BF16), (row <= col).astype(BF16)
    qkv, beta_cols, gc_cols, gc_rows, egc_cols = [], [], [], [], []
    for s in range(nseq):
        xp_ref[s, 8:8 + c, :] = x_ref[s * c:(s + 1) * c, :]
        y = xp_ref[s, 5:5 + c, :] * cw_ref[0:1, :]
        for j in range(1, CONV_W):
            y = y + xp_ref[s, 5 + j:5 + j + c, :] * cw_ref[j:j + 1, :]
        qkv.append(y * jax.nn.sigmoid(y))
        tail = xp_ref[s, 5 + c:8 + c, :]
        conv1_ref[s] = tail
        xp_ref[s, 5:8, :] = tail
        gates = g_ref[s * c:(s + 1) * c, :]
        beta_cols.append(jax.nn.sigmoid(gates[:, 2 * HA:2 * HA + HB]))
        g_cols = -jnp.exp(alog_ref[...]) * jax.nn.softplus(gates[:, 2 * HA + HB:2 * HA + 2 * HB] + dtb_ref[...])
        gc_cols.append(_mask_dot(low01, g_cols))
        gc_rows.append(_mask_dot_t(g_cols, upp01))
        egc_cols.append(jnp.exp(gc_cols[s]))
    units = [(s, h) for s in range(nseq) for h in range(HB)]
    idx = range(len(units))
    q, k, k16, kbeta, rhs, decay = [], [], [], [], [], []
    for s, h in units:
        qh = qkv[s][:, h * DK_B:(h + 1) * DK_B]
        kh = qkv[s][:, HB * DK_B + h * DK_B:HB * DK_B + (h + 1) * DK_B]
        vh = qkv[s][:, 2 * HB * DK_B + h * DV_B:2 * HB * DK_B + (h + 1) * DV_B]
        qh = qh * lax.rsqrt(jnp.sum(qh * qh, axis=-1, keepdims=True) + EPS) * DK_B ** -0.5
        kh = kh * lax.rsqrt(jnp.sum(kh * kh, axis=-1, keepdims=True) + EPS)
        beta = beta_cols[s][:, h:h + 1]
        kb = kh * beta
        q.append(qh)
        k.append(kh)
        k16.append(kh.astype(BF16))
        kbeta.append(kb)
        rhs.append(jnp.concatenate([vh * beta, kb * egc_cols[s][:, h:h + 1]], axis=1))
        decay.append(jnp.where(incl, jnp.exp(jnp.where(incl, gc_cols[s][:, h:h + 1] - gc_rows[s][h:h + 1, :], 0.0)), 0.0))
    kk = [_dot_nt(kbeta[u].astype(BF16), k16[u]) for u in idx]
    qk = [_dot_nt(q[u].astype(BF16), k16[u]) for u in idx]
    n = _unit_lower_inverses_minus_eye([jnp.where(strict, kk[u] * decay[u], 0.0) for u in idx], c)
    sol = [rhs[u] + _dot(n[u].astype(BF16), rhs[u].astype(BF16)) for u in idx]
    st = [s1_ref[s, h] for s, h in units]
    st16 = [x.astype(BF16) for x in st]
    ws = [_dot(sol[u][:, DV_B:].astype(BF16), st16[u]) for u in idx]
    qs = [_dot((q[u] * egc_cols[s][:, h:h + 1]).astype(BF16), st16[u]) for u, (s, h) in enumerate(units)]
    v_new16 = [(sol[u][:, :DV_B] - ws[u]).astype(BF16) for u in idx]
    av = [_dot((qk[u] * decay[u]).astype(BF16), v_new16[u]) for u in idx]
    upd = []
    for u, (s, h) in enumerate(units):
        kd = k[u] * jnp.exp(gc_cols[s][c - 1:c, h:h + 1] - gc_cols[s][:, h:h + 1])
        upd.append(_dot_tn(kd.astype(BF16), v_new16[u]))
    for u, (s, h) in enumerate(units):
        s1_ref[s, h] = jnp.exp(gc_cols[s][c - 1:c, h:h + 1]) * st[u] + upd[u]
        zz = z_ref[s * c:(s + 1) * c, h * DV_B:(h + 1) * DV_B]
        yo = _rms(qs[u] + av[u], nw_ref[...]) * (zz * jax.nn.sigmoid(zz))
        mix_ref[s * c:(s + 1) * c, h * DV_B:(h + 1) * DV_B] = yo.astype(mix_ref.dtype)


def _gdn(p, s0, conv0, conv_w, a_log, dt_bias, norm_w, j, *, batch, c, nc, nseq, row0, out_dtype):
    rows, out_rows, state_in, state_out, small = _seq_specs(c, nc, nseq, row0, j)
    return pl.pallas_call(
        functools.partial(_gdn_kernel, c=c, nseq=nseq),
        grid=(batch // nseq, nc),
        in_specs=[rows(GDN_QKV, OFF_BQKV), rows(GDN_V, OFF_BZ), rows(LANES, OFF_GATES),
                  state_in((HB, DK_B, DV_B)), state_in((CONV_W - 1, GDN_QKV)),
                  small((CONV_W, GDN_QKV)), small((1, HB)), small((1, HB)), small((1, DV_B))],
        out_specs=[out_rows(GDN_V), state_out((HB, DK_B, DV_B)), state_out((CONV_W - 1, GDN_QKV))],
        out_shape=[jax.ShapeDtypeStruct((batch * nc * c, GDN_V), out_dtype),
                   jax.ShapeDtypeStruct((batch, HB, DK_B, DV_B), F32),
                   jax.ShapeDtypeStruct((batch, CONV_W - 1, GDN_QKV), F32)],
        scratch_shapes=[pltpu.VMEM((nseq, c + 8, GDN_QKV), F32)],
        compiler_params=_params(("parallel", "arbitrary")),
        name="gdn",
    )(p, p, p, s0, conv0, conv_w, a_log, dt_bias, norm_w)


def _rope(x, cos, sin, first_half):
    outs = []
    for j in range(x.shape[1] // LANES):
        xs = x[:, j * LANES:(j + 1) * LANES]
        rot = jnp.where(first_half, -pltpu.roll(xs, LANES - HD_C // 2, 1), pltpu.roll(xs, HD_C // 2, 1))
        outs.append(xs * cos + rot * sin)
    return outs[0] if len(outs) == 1 else jnp.concatenate(outs, axis=1)


def _sink_softmax_terms(s, valid, sink):
    s = jnp.where(valid, s * HD_C ** -0.5, -jnp.inf)
    mx = jnp.maximum(jnp.max(s, axis=1, keepdims=True), sink)
    e = jnp.exp(s - mx)
    return e, 1.0 / (jnp.sum(e, axis=1, keepdims=True) + jnp.exp(sink - mx))


def _dup_halves(x, lo):
    out = []
    for j in range(x.shape[1] // LANES):
        slab = x[:, j * LANES:(j + 1) * LANES]
        swapped = pltpu.roll(slab, HD_C, 1)
        out += [jnp.where(lo, slab, swapped), jnp.where(lo, swapped, slab)]
    return jnp.concatenate(out, axis=1)


def _swa_prompt_kernel(q_ref, k_ref, v_ref, cos_ref, sin_ref, sink_ref, o_ref, kk_ref, vk_ref, kb_ref, vb_ref):
    nblk = pl.program_id(1)
    w = WINDOW

    @pl.when(nblk == 0)
    def _():
        kb_ref[0:w, :] = jnp.zeros((w, kb_ref.shape[1]), BF16)
        vb_ref[0:w, :] = jnp.zeros((w, vb_ref.shape[1]), BF16)

    cos, sin = cos_ref[...], sin_ref[...]
    lane = lax.broadcasted_iota(jnp.int32, (w, LANES), 1)
    first_half = (lane & (HD_C - 1)) < HD_C // 2
    lo = lane < HD_C
    q = _rope(q_ref[...], cos, sin, first_half).astype(BF16)
    k = _rope(k_ref[...], cos, sin, first_half)
    v = v_ref[...]
    kk_ref[0] = k
    vk_ref[0] = v
    kb_ref[w:2 * w, :] = _dup_halves(k, lo).astype(BF16)
    vb_ref[w:2 * w, :] = _dup_halves(v, lo).astype(BF16)
    rows = GROUP * w
    qi = lax.broadcasted_iota(jnp.int32, (rows, 2 * w), 0) & (w - 1)
    kj = lax.broadcasted_iota(jnp.int32, (rows, 2 * w), 1)
    valid = (kj > qi) & (kj <= qi + w) & ((kj >= w) | (nblk > 0))
    zero = jnp.zeros((w, LANES), BF16)
    for g0 in range(0, HKV_C, SWA_STAGE_KV):
        gs = range(g0, g0 + SWA_STAGE_KV)
        sc, sinks = [], []
        for g in gs:
            slabs = [q[:, (GROUP // 2 * g + j) * LANES:(GROUP // 2 * g + j + 1) * LANES] for j in range(GROUP // 2)]
            q4 = jnp.concatenate([jnp.where(keep, s, zero) for s in slabs for keep in (lo, ~lo)], axis=0)
            sc.append(_dot_nt(q4, kb_ref[:, g * LANES:(g + 1) * LANES]))
            sinks.append(jnp.concatenate([jnp.broadcast_to(sink_ref[:, h:h + 1], (w, 1))
                                          for h in range(g * GROUP, (g + 1) * GROUP)], axis=0))
        terms = [_sink_softmax_terms(sc[i], valid, sinks[i]) for i in range(len(gs))]
        pv = [_dot(terms[i][0].astype(BF16), vb_ref[:, g * LANES:(g + 1) * LANES]) for i, g in enumerate(gs)]
        for i, g in enumerate(gs):
            o = pv[i] * terms[i][1]
            for j in range(GROUP // 2):
                pair = jnp.where(lo, o[2 * j * w:(2 * j + 1) * w, :], o[(2 * j + 1) * w:(2 * j + 2) * w, :])
                o_ref[:, (GROUP // 2 * g + j) * LANES:(GROUP // 2 * g + j + 1) * LANES] = pair.astype(o_ref.dtype)
    kb_ref[0:w, :] = kb_ref[w:2 * w, :]
    vb_ref[0:w, :] = vb_ref[w:2 * w, :]


def _swa_prompt(p, cos, sin, sinks, *, batch, seq):
    w = WINDOW
    nb = seq // w
    nq, nkv = HQ_C * HD_C, HKV_C * HD_C
    keep = pl.BlockSpec((1, w, nkv), lambda b, i: (b, 0, 0))
    tab = pl.BlockSpec((w, LANES), lambda b, i: (i, 0))
    return pl.pallas_call(
        _swa_prompt_kernel,
        grid=(batch, nb),
        in_specs=[pl.BlockSpec((w, nq), lambda b, i: (b * nb + i, 0)),
                  pl.BlockSpec((w, nkv), lambda b, i: (b * nb + i, nq // nkv)),
                  pl.BlockSpec((w, nkv), lambda b, i: (b * nb + i, nq // nkv + 1)),
                  tab, tab, pl.BlockSpec((1, HQ_C), lambda b, i: (0, 0))],
        out_specs=[pl.BlockSpec((w, nq), lambda b, i: (b * nb + i, 0)), keep, keep],
        out_shape=[jax.ShapeDtypeStruct((batch * seq, nq), BF16),
                   jax.ShapeDtypeStruct((batch, w, nkv), F32),
                   jax.ShapeDtypeStruct((batch, w, nkv), F32)],
        scratch_shapes=[pltpu.VMEM((2 * w, HKV_C * LANES), BF16), pltpu.VMEM((2 * w, HKV_C * LANES), BF16)],
        compiler_params=_params(("parallel", "arbitrary")),
        name="swa_prompt",
    )(p, p, p, cos, sin, sinks)


def _swa_sample_kernel(q_ref, k_ref, v_ref, kb_ref, vb_ref, cos_ref, sin_ref, sink_ref,
                       o_ref, kk_ref, vk_ref, kc_ref, vc_ref, *, seq):
    w = WINDOW
    span = 2 * w
    cos, sin = cos_ref[...], sin_ref[...]
    first_half = (lax.broadcasted_iota(jnp.int32, (seq, LANES), 1) & (HD_C - 1)) < HD_C // 2
    q = _rope(q_ref[...], cos, sin, first_half).astype(BF16)
    kc_ref[0:w, :] = kb_ref[0]
    vc_ref[0:w, :] = vb_ref[0]
    kc_ref[w:w + seq, :] = _rope(k_ref[...], cos, sin, first_half)
    vc_ref[w:w + seq, :] = v_ref[...]
    kc_ref[w + seq:span, :] = jnp.zeros((span - w - seq, kc_ref.shape[1]), F32)
    vc_ref[w + seq:span, :] = jnp.zeros((span - w - seq, vc_ref.shape[1]), F32)
    kk_ref[0] = kc_ref[seq:seq + w, :]
    vk_ref[0] = vc_ref[seq:seq + w, :]
    kcat, vcat = kc_ref[...].astype(BF16), vc_ref[...].astype(BF16)
    rows = GROUP * seq
    qi = lax.broadcasted_iota(jnp.int32, (rows, span), 0) & (seq - 1)
    kj = lax.broadcasted_iota(jnp.int32, (rows, span), 1)
    valid = (kj > qi) & (kj <= qi + w) & (kj + (PAST_LEN - w) >= 0)
    groups = range(HKV_C)
    heads = [range(g * GROUP, (g + 1) * GROUP) for g in groups]
    sc = [_dot_nt(jnp.concatenate([q[:, h * HD_C:(h + 1) * HD_C] for h in heads[g]], axis=0),
                  kcat[:, g * HD_C:(g + 1) * HD_C]) for g in groups]
    terms = []
    for g in groups:
        sink = jnp.concatenate([jnp.broadcast_to(sink_ref[:, h:h + 1], (seq, 1)) for h in heads[g]], axis=0)
        terms.append(_sink_softmax_terms(sc[g], valid, sink))
    pv = [_dot(terms[g][0].astype(BF16), vcat[:, g * HD_C:(g + 1) * HD_C]) for g in groups]
    for g in groups:
        o = pv[g] * terms[g][1]
        for hh, h in enumerate(heads[g]):
            o_ref[:, h * HD_C:(h + 1) * HD_C] = o[hh * seq:(hh + 1) * seq, :]


def _swa_sample(p, k_buf, v_buf, cos, sin, sinks, *, batch, seq, row0):
    w = WINDOW
    nq, nkv = HQ_C * HD_C, HKV_C * HD_C
    rb = row0 // seq
    keep = pl.BlockSpec((1, w, nkv), lambda b: (b, 0, 0))
    tab = pl.BlockSpec((seq, LANES), lambda b: (0, 0))
    return pl.pallas_call(
        functools.partial(_swa_sample_kernel, seq=seq),
        grid=(batch,),
        in_specs=[pl.BlockSpec((seq, nq), lambda b: (rb + b, 0)),
                  pl.BlockSpec((seq, nkv), lambda b: (rb + b, nq // nkv)),
                  pl.BlockSpec((seq, nkv), lambda b: (rb + b, nq // nkv + 1)),
                  keep, keep, tab, tab, pl.BlockSpec((1, HQ_C), lambda b: (0, 0))],
        out_specs=[pl.BlockSpec((seq, nq), lambda b: (b, 0)), keep, keep],
        out_shape=[jax.ShapeDtypeStruct((batch * seq, nq), F32),
                   jax.ShapeDtypeStruct((batch, w, nkv), F32),
                   jax.ShapeDtypeStruct((batch, w, nkv), F32)],
        scratch_shapes=[pltpu.VMEM((2 * w, nkv), F32), pltpu.VMEM((2 * w, nkv), F32)],
        compiler_params=_params(("parallel",)),
        name="swa_sample",
    )(p, p, p, k_buf, v_buf, cos, sin, sinks)


def _rope_tables(pos):
    half = HD_C // 2
    inv = ROPE_THETA ** (-jnp.arange(half, dtype=F32) / half)
    ang = pos.astype(F32)[:, None] * inv[None, :]
    reps = LANES // half
    return jnp.tile(jnp.cos(ang), (1, reps)), jnp.tile(jnp.sin(ang), (1, reps))


def kernel(x_prompt, x_sample, state_mlstm_C, state_mlstm_n, state_mlstm_m, state_gdn_S, state_gdn_conv,
           cache_swa_k, cache_swa_v, norm_w, ffn_w_gate, ffn_w_up, ffn_w_down, ab_w_in, ab_w_out,
           mlstm_b_i, mlstm_b_f, mlstm_norm_w, gdn_conv_w, gdn_a_log, gdn_dt_bias, gdn_norm_w,
           c_w_qkv, c_w_out, c_sinks):
    bp, lp, d = x_prompt.shape
    bs, ls, _ = x_sample.shape
    mp, ms = bp * lp, bs * ls
    x = jnp.concatenate([x_prompt.reshape(mp, d), x_sample.reshape(ms, d)], axis=0)

    wg = _cast_pad(ffn_w_gate.reshape(DEPTH * 2, d, FFN_DIM), d, FFN_DIM, split_rows=True)
    wu = _cast_pad(ffn_w_up.reshape(DEPTH * 2, d, FFN_DIM), d, FFN_DIM, split_rows=True)
    wd = _cast_pad(ffn_w_down.reshape(DEPTH * 2, FFN_DIM, d), FFN_DIM, d, split_rows=False)
    w_ab = _perm_ab(ab_w_in)
    w_ab_out = _cast_pad(ab_w_out, ab_w_out.shape[1], d, split_rows=True)
    w_c = _cast_pad(c_w_qkv, d, c_w_qkv.shape[2], split_rows=True)
    w_c_out = _cast_pad(c_w_out, c_w_out.shape[1], d, split_rows=True)

    cos_p, sin_p = _rope_tables(jnp.arange(lp, dtype=jnp.int32))
    cos_s, sin_s = _rope_tables(PAST_LEN + jnp.arange(ls, dtype=jnp.int32))
    nkv = HKV_C * HD_C

    nw = norm_w.reshape(DEPTH * 6, 1, d)
    n_even = mlstm_b_i.shape[0]
    zeros = lambda *shape: jnp.zeros((n_even,) + shape, F32)
    mlstm_par = (mlstm_b_i[:, None, :], mlstm_b_f[:, None, :], mlstm_norm_w)
    gdn_par = (gdn_conv_w, gdn_a_log[:, None, :], gdn_dt_bias[:, None, :], gdn_norm_w[:, None, :])
    rec_p, rec_s, win_p, win_s = [], [], [], []
    for layer in range(DEPTH):
        j = layer // 2
        x = _ffn(x, nw, wg, wu, wd, layer, 0)
        if layer % 2 == 0:
            p = _proj_in(x, nw, layer * 6 + 2, w_ab, j)
            prompt = dict(batch=bp, c=CHUNK, nc=lp // CHUNK, nseq=1, row0=0, out_dtype=BF16)
            sample = dict(batch=bs, c=ls, nc=1, nseq=SAMPLE_NSEQ, row0=mp, out_dtype=F32)
            ma_p, c1, n1, m1 = _mlstm(p, zeros(bp, HA, DK_A, DV_A), zeros(bp, HA, DK_A), zeros(bp, 1, HA),
                                      *mlstm_par, j, **prompt)
            mb_p, s1, cv1 = _gdn(p, zeros(bp, HB, DK_B, DV_B), zeros(bp, CONV_W - 1, GDN_QKV), *gdn_par, j, **prompt)
            rec_p.append((c1, n1, m1.reshape(bp, HA), s1, cv1))
            ma_s, c1, n1, m1 = _mlstm(p, state_mlstm_C, state_mlstm_n, state_mlstm_m[:, :, None, :],
                                      *mlstm_par, j, **sample)
            mb_s, s1, cv1 = _gdn(p, state_gdn_S, state_gdn_conv, *gdn_par, j, **sample)
            rec_s.append((c1, n1, m1.reshape(bs, HA), s1, cv1))
            x = _proj_out([ma_p, mb_p], [ma_s.astype(BF16), mb_s.astype(BF16)], w_ab_out, j, x, nw, layer * 6 + 3)
        else:
            p = _proj_in(x, nw, layer * 6 + 2, w_c, j)
            sinks = c_sinks[j][None, :]
            o_p, kk, vk = _swa_prompt(p, cos_p, sin_p, sinks, batch=bp, seq=lp)
            win_p.append((kk.reshape(bp, WINDOW, HKV_C, HD_C), vk.reshape(bp, WINDOW, HKV_C, HD_C)))
            o_s, kk, vk = _swa_sample(p, cache_swa_k[j].reshape(bs, WINDOW, nkv),
                                      cache_swa_v[j].reshape(bs, WINDOW, nkv), cos_s, sin_s, sinks,
                                      batch=bs, seq=ls, row0=mp)
            win_s.append((kk.reshape(bs, WINDOW, HKV_C, HD_C), vk.reshape(bs, WINDOW, HKV_C, HD_C)))
            x = _proj_out([o_p], [o_s.astype(BF16)], w_c_out, j, x, nw, layer * 6 + 3)
        x = _ffn(x, nw, wg, wu, wd, layer, 1, head_rows=mp if layer == DEPTH - 1 else None)

    stack = lambda sts, i: jnp.stack([st[i] for st in sts])
    y_prompt = x[0].reshape(bp, lp, d)
    y_sample = x[1].reshape(bs, ls, d)
    return ((y_prompt, y_sample)
            + tuple(stack(rec_p, i) for i in range(5)) + tuple(stack(win_p, i) for i in range(2))
            + tuple(stack(rec_s, i) for i in range(5)) + tuple(stack(win_s, i) for i in range(2)))
```

```python
import functools

import jax
import jax.numpy as jnp
from jax import lax
from jax.experimental import pallas as pl
from jax.experimental.pallas import tpu as pltpu

F32 = jnp.float32
BF16 = jnp.bfloat16

D_MODEL = 2048
DEPTH = 4
PAST_LEN = 16384
HA, DK_A, DV_A = 4, 128, 256
A_QK, A_V = HA * DK_A, HA * DV_A
GATE_CAP = 15.0
HB, DK_B, DV_B = 8, 128, 128
GDN_QKV, GDN_V = HB * (2 * DK_B + DV_B), HB * DV_B
CONV_W = 4
HQ_C, HKV_C, HD_C = 32, 8, 64
GROUP = HQ_C // HKV_C
WINDOW = 128
ROPE_THETA = 10000.0
FFN_DIM = 5504
EPS = 1e-6

LANES = 128
FFN_TM, FFN_TF = 768, 512
PROJ_TM, PROJ_TN = 768, 1536
CAST_TILE = 256
OUT_TM = 256
CHUNK = 128
SAMPLE_NSEQ = 4
SWA_STAGE_KV = 2
OFF_AQ, OFF_AK, OFF_AV, OFF_AO = 0, A_QK, 2 * A_QK, 2 * A_QK + A_V
OFF_BQKV = 2 * A_QK + 2 * A_V
OFF_BZ = OFF_BQKV + GDN_QKV
OFF_GATES = OFF_BZ + GDN_V
IN_AB_PAD = 7680
VMEM_LIMIT = 56 * 1024 * 1024


def _rms(x, w):
    return x * lax.rsqrt(jnp.mean(x * x, axis=-1, keepdims=True) + EPS) * w


def _dot(a, b):
    return jnp.dot(a, b, preferred_element_type=F32)


def _dot_nt(a, b):
    return lax.dot_general(a, b, (((1,), (1,)), ((), ())), preferred_element_type=F32)


def _dot_tn(a, b):
    return lax.dot_general(a, b, (((0,), (0,)), ((), ())), preferred_element_type=F32)


def _params(sem):
    return pltpu.CompilerParams(dimension_semantics=sem, vmem_limit_bytes=VMEM_LIMIT)


def _cast_pad_kernel(x_ref, o_ref):
    r, n = x_ref.shape
    o_ref[:r, :n] = x_ref[...].astype(BF16)
    if o_ref.shape[1] > n:
        o_ref[:, n:] = jnp.zeros((o_ref.shape[0], o_ref.shape[1] - n), BF16)
    if o_ref.shape[0] > r:
        o_ref[r:, :] = jnp.zeros((o_ref.shape[0] - r, o_ref.shape[1]), BF16)


def _cast_pad(w, rows_out, cols_out, *, split_rows):
    l, r, n = w.shape
    if split_rows:
        assert rows_out == r and r % CAST_TILE == 0
        grid = (l, r // CAST_TILE)
        in_spec = pl.BlockSpec((None, CAST_TILE, n), lambda a, b: (a, b, 0))
        out_spec = pl.BlockSpec((None, CAST_TILE, cols_out), lambda a, b: (a, b, 0))
    else:
        assert cols_out == n and n % CAST_TILE == 0
        grid = (l, n // CAST_TILE)
        in_spec = pl.BlockSpec((None, r, CAST_TILE), lambda a, b: (a, 0, b))
        out_spec = pl.BlockSpec((None, rows_out, CAST_TILE), lambda a, b: (a, 0, b))
    return pl.pallas_call(
        _cast_pad_kernel, grid=grid, in_specs=[in_spec], out_specs=out_spec,
        out_shape=jax.ShapeDtypeStruct((l, rows_out, cols_out), BF16),
        compiler_params=_params(("parallel", "parallel")), name="cast_pad",
    )(w)


def _perm_ab_kernel(x_ref, o_ref):
    o_ai = OFF_BQKV
    o_bq = o_ai + 2 * HA
    o_bb = o_bq + GDN_QKV + GDN_V
    n_gates = 2 * HA + 2 * HB
    o_ref[:OFF_BQKV, :] = x_ref[:o_ai, :].astype(BF16)
    o_ref[OFF_BQKV:OFF_GATES, :] = x_ref[o_bq:o_bb, :].astype(BF16)
    o_ref[OFF_GATES:, :] = jnp.zeros((IN_AB_PAD - OFF_GATES, o_ref.shape[1]), BF16)
    o_ref[OFF_GATES:OFF_GATES + 2 * HA, :] = x_ref[o_ai:o_bq, :].astype(BF16)
    o_ref[OFF_GATES + 2 * HA:OFF_GATES + n_gates, :] = x_ref[o_bb:, :].astype(BF16)


def _perm_ab(wt):
    l, n, r = wt.shape
    return pl.pallas_call(
        _perm_ab_kernel, grid=(l, r // CAST_TILE),
        in_specs=[pl.BlockSpec((None, n, CAST_TILE), lambda a, b: (a, 0, b))],
        out_specs=pl.BlockSpec((None, IN_AB_PAD, CAST_TILE), lambda a, b: (a, 0, b)),
        out_shape=jax.ShapeDtypeStruct((l, IN_AB_PAD, r), BF16),
        compiler_params=_params(("parallel", "parallel")), name="perm_ab",
    )(wt)


def _ffn_kernel(*refs, nf_main, n_tail, tail_rows):
    x_ref, nwa_ref, nwb_ref, wg_ref, wu_ref, wd_ref = refs[:6]
    wgt, wut, wdt = (refs[6 + k * n_tail:6 + (k + 1) * n_tail] for k in range(3))
    rest = refs[6 + 3 * n_tail:]
    o_ref, h_ref = rest[0], rest[-1]
    f = pl.program_id(1)
    last = nf_main if n_tail else nf_main - 1

    def partial_out(wg, wu, wd):
        h = h_ref[...]
        g = _dot(h, wg)
        u = _dot(h, wu)
        return _dot((g * jax.nn.sigmoid(g) * u).astype(BF16), wd)

    def finish(acc):
        o_ref[...] = x_ref[...] + 0.5 * _rms(acc, nwb_ref[...])
        if tail_rows:
            @pl.when(pl.program_id(0) == pl.num_programs(0) - 1)
            def _():
                rest[1][...] = o_ref[o_ref.shape[0] - tail_rows:, :]

    @pl.when(f == 0)
    def _():
        h_ref[...] = _rms(x_ref[...], nwa_ref[...]).astype(BF16)
        o_ref[...] = partial_out(wg_ref[...], wu_ref[...], wd_ref[...])

    @pl.when((f > 0) & (f < last))
    def _():
        o_ref[...] += partial_out(wg_ref[...], wu_ref[...], wd_ref[...])

    @pl.when(f == last)
    def _():
        if n_tail:
            cat = lambda rs, axis: jnp.concatenate([r[...] for r in rs], axis=axis)
            finish(o_ref[...] + partial_out(cat(wgt, 1), cat(wut, 1), cat(wdt, 0)))
        else:
            finish(o_ref[...] + partial_out(wg_ref[...], wu_ref[...], wd_ref[...]))


def _ffn(x, nw, wg, wu, wd, layer, half, head_rows=None):
    m, d = x.shape
    fdim = wg.shape[-1]
    nf_main, n_tail = fdim // FFN_TF, (fdim % FFN_TF) // LANES
    assert nf_main >= 2 and fdim == nf_main * FFN_TF + n_tail * LANES
    pre = layer * 6 + 4 * half
    wi = layer * 2 + half
    row = pl.BlockSpec((FFN_TM, d), lambda i, f: (i, 0))
    vec = lambda r: pl.BlockSpec((None, 1, d), lambda i, f: (r, 0, 0))
    main = lambda f: jnp.minimum(f, nf_main - 1)
    wcol = pl.BlockSpec((None, d, FFN_TF), lambda i, f: (wi, 0, main(f)))
    wrow = pl.BlockSpec((None, FFN_TF, d), lambda i, f: (wi, main(f), 0))
    t0 = nf_main * FFN_TF // LANES
    tcol = [pl.BlockSpec((None, d, LANES), functools.partial(lambda i, f, t: (wi, 0, t0 + t), t=t)) for t in range(n_tail)]
    trow = [pl.BlockSpec((None, LANES, d), functools.partial(lambda i, f, t: (wi, t0 + t, 0), t=t)) for t in range(n_tail)]
    tail_rows = 0 if head_rows is None else m - head_rows
    assert tail_rows <= FFN_TM
    out_specs, out_shape = row, jax.ShapeDtypeStruct((m, d), F32)
    if tail_rows:
        out_specs = [row, pl.BlockSpec((tail_rows, d), lambda i, f: (0, 0))]
        out_shape = [jax.ShapeDtypeStruct((head_rows, d), F32), jax.ShapeDtypeStruct((tail_rows, d), F32)]
    return pl.pallas_call(
        functools.partial(_ffn_kernel, nf_main=nf_main, n_tail=n_tail, tail_rows=tail_rows),
        grid=(m // FFN_TM, nf_main + (1 if n_tail else 0)),
        in_specs=[row, vec(pre), vec(pre + 1), wcol, wcol, wrow] + tcol + tcol + trow,
        out_specs=out_specs,
        out_shape=out_shape,
        scratch_shapes=[pltpu.VMEM((FFN_TM, d), BF16)],
        compiler_params=_params(("parallel", "arbitrary")),
        name="ffn",
    )(x, nw, nw, wg, wu, wd, *([wg] * n_tail), *([wu] * n_tail), *([wd] * n_tail))


def _proj_in_kernel(x_ref, nw_ref, w_ref, o_ref, h_ref, *, w_transposed):
    mm = _dot_nt if w_transposed else _dot

    @pl.when(pl.program_id(1) == 0)
    def _():
        h_ref[...] = _rms(x_ref[...], nw_ref[...]).astype(BF16)
        o_ref[...] = mm(h_ref[...], w_ref[...])

    @pl.when(pl.program_id(1) != 0)
    def _():
        o_ref[...] = mm(h_ref[...], w_ref[...])


def _proj_in(x, nw, nw_row, w, j, w_transposed=False):
    m, d = x.shape
    n = w.shape[1] if w_transposed else w.shape[2]
    w_spec = (pl.BlockSpec((None, PROJ_TN, d), lambda i, k: (j, k, 0)) if w_transposed
              else pl.BlockSpec((None, d, PROJ_TN), lambda i, k: (j, 0, k)))
    return pl.pallas_call(
        functools.partial(_proj_in_kernel, w_transposed=w_transposed),
        grid=(m // PROJ_TM, n // PROJ_TN),
        in_specs=[pl.BlockSpec((PROJ_TM, d), lambda i, k: (i, 0)),
                  pl.BlockSpec((None, 1, d), lambda i, k: (nw_row, 0, 0)),
                  w_spec],
        out_specs=pl.BlockSpec((PROJ_TM, PROJ_TN), lambda i, k: (i, k)),
        out_shape=jax.ShapeDtypeStruct((m, n), F32),
        scratch_shapes=[pltpu.VMEM((PROJ_TM, d), BF16)],
        compiler_params=_params(("parallel", "arbitrary")),
        name="proj_in",
    )(x, nw, w)


def _proj_out_kernel(*refs, n_parts, prompt_tiles):
    p_refs, s_refs, w_refs = refs[:n_parts], refs[n_parts:2 * n_parts], refs[2 * n_parts:3 * n_parts]
    x_ref, nw_ref, o_ref = refs[3 * n_parts:]

    def emit(a_refs):
        y = _dot(a_refs[0][...], w_refs[0][...])
        for a_ref, w_ref in zip(a_refs[1:], w_refs[1:]):
            y += _dot(a_ref[...], w_ref[...])
        o_ref[...] = x_ref[...] + _rms(y, nw_ref[...])

    is_prompt = pl.program_id(0) < prompt_tiles
    pl.when(is_prompt)(lambda: emit(p_refs))
    pl.when(jnp.logical_not(is_prompt))(lambda: emit(s_refs))


def _proj_out(p_parts, s_parts, w, j, x, nw, nw_row):
    m, d = x.shape
    n_parts = len(p_parts)
    mp, ms = p_parts[0].shape[0], s_parts[0].shape[0]
    assert mp % OUT_TM == 0 and ms == OUT_TM and mp + ms == m
    prompt_tiles = mp // OUT_TM
    kw = [p.shape[1] for p in p_parts]
    assert len(set(kw)) == 1 and sum(kw) == w.shape[1]
    in_specs = [pl.BlockSpec((OUT_TM, k), lambda i: (jnp.minimum(i, prompt_tiles - 1), 0)) for k in kw]
    in_specs += [pl.BlockSpec((OUT_TM, k), lambda i: (0, 0)) for k in kw]
    in_specs += [pl.BlockSpec((None, k, d), functools.partial(lambda i, part: (j, part, 0), part=part))
                 for part, k in enumerate(kw)]
    in_specs += [pl.BlockSpec((OUT_TM, d), lambda i: (i, 0)), pl.BlockSpec((None, 1, d), lambda i: (nw_row, 0, 0))]
    return pl.pallas_call(
        functools.partial(_proj_out_kernel, n_parts=n_parts, prompt_tiles=prompt_tiles),
        grid=(m // OUT_TM,),
        in_specs=in_specs,
        out_specs=pl.BlockSpec((OUT_TM, d), lambda i: (i, 0)),
        out_shape=jax.ShapeDtypeStruct((m, d), F32),
        compiler_params=_params(("parallel",)),
        name="proj_out",
    )(*p_parts, *s_parts, *([w] * n_parts), x, nw)


def _iota2(c):
    return lax.broadcasted_iota(jnp.int32, (c, c), 0), lax.broadcasted_iota(jnp.int32, (c, c), 1)


def _split3(x):
    a = x.astype(BF16)
    r = x - a.astype(F32)
    b = r.astype(BF16)
    return a, b, (r - b.astype(F32)).astype(BF16)


def _mask_dot(mask01, x):
    a, b, d = _split3(x)
    return _dot(mask01, a) + _dot(mask01, b) + _dot(mask01, d)


def _mask_dot_t(x, mask01):
    a, b, d = _split3(x)
    return _dot_tn(a, mask01) + _dot_tn(b, mask01) + _dot_tn(d, mask01)


def _softcap(x, cap):
    return cap * jnp.tanh(x / cap)


def _mlstm_kernel(q_ref, k_ref, v_ref, og_ref, g_ref, c0_ref, n0_ref, m0_ref, bi_ref, bf_ref, nw_ref,
                  mix_ref, c1_ref, n1_ref, m1_ref, *, c, nseq):
    @pl.when(pl.program_id(1) == 0)
    def _():
        c1_ref[...] = c0_ref[...]
        n1_ref[...] = n0_ref[...]
        m1_ref[...] = m0_ref[...]

    row, col = _iota2(c)
    causal = col <= row
    eye01 = (row == col).astype(BF16)
    low01 = causal.astype(BF16)
    upp01 = (row <= col).astype(BF16)
    ig_cols, b_cols, b_rows, ig_rows = [], [], [], []
    for s in range(nseq):
        gates = g_ref[s * c:(s + 1) * c, :]
        ig = _softcap(gates[:, 0:HA] + bi_ref[...], GATE_CAP)
        lf = jax.nn.log_sigmoid(_softcap(gates[:, HA:2 * HA] + bf_ref[...], GATE_CAP))
        ig_cols.append(ig)
        b_cols.append(_mask_dot(low01, lf))
        b_rows.append(_mask_dot_t(lf, upp01))
        ig_rows.append(_mask_dot_t(ig, eye01))
    units = [(s, h) for s in range(nseq) for h in range(HA)]
    idx = range(len(units))
    rows = lambda s: slice(s * c, (s + 1) * c)
    q = [q_ref[rows(s), h * DK_A:(h + 1) * DK_A] * DK_A ** -0.5 for s, h in units]
    k = [k_ref[rows(s), h * DK_A:(h + 1) * DK_A] for s, h in units]
    q16 = [x.astype(BF16) for x in q]
    k16 = [x.astype(BF16) for x in k]
    v16 = [v_ref[rows(s), h * DV_A:(h + 1) * DV_A].astype(BF16) for s, h in units]
    cp = [c1_ref[s, h] for s, h in units]
    npv = [n1_ref[s, h:h + 1, :] for s, h in units]
    qk = [_dot_nt(q16[u], k16[u]) for u in idx]
    qc = [_dot(q16[u], cp[u].astype(BF16)) for u in idx]
    sc, sc16, kw, m, w_inter, decay = [], [], [], [], [], []
    for u, (s, h) in enumerate(units):
        b_col, ig_col = b_cols[s][:, h:h + 1], ig_cols[s][:, h:h + 1]
        dmat = jnp.where(causal, b_col - b_rows[s][h:h + 1, :] + ig_rows[s][h:h + 1, :], -jnp.inf)
        inter = b_col + m1_ref[s, :, h:h + 1]
        mh = jnp.maximum(inter, jnp.max(dmat, axis=1, keepdims=True))
        su = qk[u] * jnp.exp(dmat - mh)
        m_new = mh[c - 1:c, :]
        kw.append(k[u] * jnp.exp(b_col[c - 1:c, :] - b_col + ig_col - m_new))
        decay.append(jnp.exp(inter[c - 1:c, :] - m_new))
        w_inter.append(jnp.exp(inter - mh))
        m.append(mh)
        sc.append(su)
        sc16.append(su.astype(BF16))
    sv = [_dot(sc16[u], v16[u]) for u in idx]
    upd = [_dot_tn(kw[u].astype(BF16), v16[u]) for u in idx]
    for u, (s, h) in enumerate(units):
        num = w_inter[u] * qc[u] + sv[u]
        den = (w_inter[u] * jnp.sum(q[u] * npv[u], axis=1, keepdims=True)
               + jnp.sum(sc[u], axis=1, keepdims=True))
        hh = num / jnp.maximum(jnp.abs(den), jnp.exp(-m[u]))
        c1_ref[s, h] = decay[u] * cp[u] + upd[u]
        n1_ref[s, h:h + 1, :] = decay[u] * npv[u] + jnp.sum(kw[u], axis=0, keepdims=True)
        m1_ref[s, :, h:h + 1] = m[u][c - 1:c, :]
        y = _rms(hh, nw_ref[h:h + 1, :]) * jax.nn.sigmoid(og_ref[rows(s), h * DV_A:(h + 1) * DV_A])
        mix_ref[rows(s), h * DV_A:(h + 1) * DV_A] = y.astype(mix_ref.dtype)


def _seq_specs(c, nc, nseq, row0, j):
    assert nseq == 1 or nc == 1
    rb = row0 // (nseq * c)
    assert rb * nseq * c == row0
    rows = lambda width, off: pl.BlockSpec((nseq * c, width), lambda b, i: (rb + b * nc + i, off // width))
    out_rows = lambda width: pl.BlockSpec((nseq * c, width), lambda b, i: (b * nc + i, 0))
    state_in = lambda shape: pl.BlockSpec((None, nseq) + shape, lambda b, i: (j, b) + (0,) * len(shape))
    state_out = lambda shape: pl.BlockSpec((nseq,) + shape, lambda b, i: (b,) + (0,) * len(shape))
    small = lambda shape: pl.BlockSpec((None,) + shape, lambda b, i: (j,) + (0,) * len(shape))
    return rows, out_rows, state_in, state_out, small


def _mlstm(p, c0, n0, m0, b_i, b_f, norm_w, j, *, batch, c, nc, nseq, row0, out_dtype):
    rows, out_rows, state_in, state_out, small = _seq_specs(c, nc, nseq, row0, j)
    return pl.pallas_call(
        functools.partial(_mlstm_kernel, c=c, nseq=nseq),
        grid=(batch // nseq, nc),
        in_specs=[rows(A_QK, OFF_AQ), rows(A_QK, OFF_AK), rows(A_V, OFF_AV), rows(A_V, OFF_AO),
                  rows(LANES, OFF_GATES),
                  state_in((HA, DK_A, DV_A)), state_in((HA, DK_A)), state_in((1, HA)),
                  small((1, HA)), small((1, HA)), small((HA, DV_A))],
        out_specs=[out_rows(A_V), state_out((HA, DK_A, DV_A)), state_out((HA, DK_A)), state_out((1, HA))],
        out_shape=[jax.ShapeDtypeStruct((batch * nc * c, A_V), out_dtype),
                   jax.ShapeDtypeStruct((batch, HA, DK_A, DV_A), F32),
                   jax.ShapeDtypeStruct((batch, HA, DK_A), F32),
                   jax.ShapeDtypeStruct((batch, 1, HA), F32)],
        compiler_params=_params(("parallel", "arbitrary")),
        name="mlstm",
    )(p, p, p, p, p, c0, n0, m0, b_i, b_f, norm_w)


def _unit_lower_inverses_minus_eye(mats, c):
    row, col = _iota2(c)
    nb = min(c, 16)
    sh = nb.bit_length() - 1
    diag_blk = (row >> sh) == (col >> sh)
    mm = lambda x, y: _dot(x.astype(BF16), y.astype(BF16))
    p = [jnp.where(diag_blk, a, 0.0) for a in mats]
    n = [-x for x in p]
    k = 1
    while 2 * k < nb:
        p = [mm(x, x) for x in p]
        np_ = [mm(x, y) for x, y in zip(n, p)]
        n = [x + y + z for x, y, z in zip(n, p, np_)]
        k *= 2
    s = nb
    while s < c:
        sh = s.bit_length() - 1
        off_blk = ((row >> (sh + 1)) == (col >> (sh + 1))) & ((row >> sh) != (col >> sh))
        off = [jnp.where(off_blk, a, 0.0) for a in mats]
        x = [o + mm(y, o) for o, y in zip(off, n)]
        xn = [mm(y, z) for y, z in zip(x, n)]
        n = [z - (y + w) for z, y, w in zip(n, x, xn)]
        s *= 2
    return n


def _gdn_kernel(x_ref, z_ref, g_ref, s0_ref, conv0_ref, cw_ref, alog_ref, dtb_ref, nw_ref,
                mix_ref, s1_ref, conv1_ref, xp_ref, *, c, nseq):
    @pl.when(pl.program_id(1) == 0)
    def _():
        s1_ref[...] = s0_ref[...]
        xp_ref[:, 5:8, :] = conv0_ref[...]

    row, col = _iota2(c)
    incl = col <= row
    strict = col < row
    low01, upp01 = incl.astype(BF16), (row <= col).astype(BF16)
    qkv, beta_cols, gc_cols, gc_rows, egc_cols = [], [], [], [], []
    for s in range(nseq):
        xp_ref[s, 8:8 + c, :] = x_ref[s * c:(s + 1) * c, :]
        y = xp_ref[s, 5:5 + c, :] * cw_ref[0:1, :]
        for j in range(1, CONV_W):
            y = y + xp_ref[s, 5 + j:5 + j + c, :] * cw_ref[j:j + 1, :]
        qkv.append(y * jax.nn.sigmoid(y))
        tail = xp_ref[s, 5 + c:8 + c, :]
        conv1_ref[s] = tail
        xp_ref[s, 5:8, :] = tail
        gates = g_ref[s * c:(s + 1) * c, :]
        beta_cols.append(jax.nn.sigmoid(gates[:, 2 * HA:2 * HA + HB]))
        g_cols = -jnp.exp(alog_ref[...]) * jax.nn.softplus(gates[:, 2 * HA + HB:2 * HA + 2 * HB] + dtb_ref[...])
        gc_cols.append(_mask_dot(low01, g_cols))
        gc_rows.append(_mask_dot_t(g_cols, upp01))
        egc_cols.append(jnp.exp(gc_cols[s]))
    units = [(s, h) for s in range(nseq) for h in range(HB)]
    idx = range(len(units))
    q, k, k16, kbeta, rhs, decay = [], [], [], [], [], []
    for s, h in units:
        qh = qkv[s][:, h * DK_B:(h + 1) * DK_B]
        kh = qkv[s][:, HB * DK_B + h * DK_B:HB * DK_B + (h + 1) * DK_B]
        vh = qkv[s][:, 2 * HB * DK_B + h * DV_B:2 * HB * DK_B + (h + 1) * DV_B]
        qh = qh * lax.rsqrt(jnp.sum(qh * qh, axis=-1, keepdims=True) + EPS) * DK_B ** -0.5
        kh = kh * lax.rsqrt(jnp.sum(kh * kh, axis=-1, keepdims=True) + EPS)
        beta = beta_cols[s][:, h:h + 1]
        kb = kh * beta
        q.append(qh)
        k.append(kh)
        k16.append(kh.astype(BF16))
        kbeta.append(kb)
        rhs.append(jnp.concatenate([vh * beta, kb * egc_cols[s][:, h:h + 1]], axis=1))
        decay.append(jnp.where(incl, jnp.exp(jnp.where(incl, gc_cols[s][:, h:h + 1] - gc_rows[s][h:h + 1, :], 0.0)), 0.0))
    kk = [_dot_nt(kbeta[u].astype(BF16), k16[u]) for u in idx]
    qk = [_dot_nt(q[u].astype(BF16), k16[u]) for u in idx]
    n = _unit_lower_inverses_minus_eye([jnp.where(strict, kk[u] * decay[u], 0.0) for u in idx], c)
    sol = [rhs[u] + _dot(n[u].astype(BF16), rhs[u].astype(BF16)) for u in idx]
    st = [s1_ref[s, h] for s, h in units]
    st16 = [x.astype(BF16) for x in st]
    ws = [_dot(sol[u][:, DV_B:].astype(BF16), st16[u]) for u in idx]
    qs = [_dot((q[u] * egc_cols[s][:, h:h + 1]).astype(BF16), st16[u]) for u, (s, h) in enumerate(units)]
    v_new16 = [(sol[u][:, :DV_B] - ws[u]).astype(BF16) for u in idx]
    av = [_dot((qk[u] * decay[u]).astype(BF16), v_new16[u]) for u in idx]
    upd = []
    for u, (s, h) in enumerate(units):
        kd = k[u] * jnp.exp(gc_cols[s][c - 1:c, h:h + 1] - gc_cols[s][:, h:h + 1])
        upd.append(_dot_tn(kd.astype(BF16), v_new16[u]))
    for u, (s, h) in enumerate(units):
        s1_ref[s, h] = jnp.exp(gc_cols[s][c - 1:c, h:h + 1]) * st[u] + upd[u]
        zz = z_ref[s * c:(s + 1) * c, h * DV_B:(h + 1) * DV_B]
        yo = _rms(qs[u] + av[u], nw_ref[...]) * (zz * jax.nn.sigmoid(zz))
        mix_ref[s * c:(s + 1) * c, h * DV_B:(h + 1) * DV_B] = yo.astype(mix_ref.dtype)


def _gdn(p, s0, conv0, conv_w, a_log, dt_bias, norm_w, j, *, batch, c, nc, nseq, row0, out_dtype):
    rows, out_rows, state_in, state_out, small = _seq_specs(c, nc, nseq, row0, j)
    return pl.pallas_call(
        functools.partial(_gdn_kernel, c=c, nseq=nseq),
        grid=(batch // nseq, nc),
        in_specs=[rows(GDN_QKV, OFF_BQKV), rows(GDN_V, OFF_BZ), rows(LANES, OFF_GATES),
                  state_in((HB, DK_B, DV_B)), state_in((CONV_W - 1, GDN_QKV)),
                  small((CONV_W, GDN_QKV)), small((1, HB)), small((1, HB)), small((1, DV_B))],
        out_specs=[out_rows(GDN_V), state_out((HB, DK_B, DV_B)), state_out((CONV_W - 1, GDN_QKV))],
        out_shape=[jax.ShapeDtypeStruct((batch * nc * c, GDN_V), out_dtype),
                   jax.ShapeDtypeStruct((batch, HB, DK_B, DV_B), F32),
                   jax.ShapeDtypeStruct((batch, CONV_W - 1, GDN_QKV), F32)],
        scratch_shapes=[pltpu.VMEM((nseq, c + 8, GDN_QKV), F32)],
        compiler_params=_params(("parallel", "arbitrary")),
        name="gdn",
    )(p, p, p, s0, conv0, conv_w, a_log, dt_bias, norm_w)


def _rope(x, cos, sin, first_half):
    outs = []
    for j in range(x.shape[1] // LANES):
        xs = x[:, j * LANES:(j + 1) * LANES]
        rot = jnp.where(first_half, -pltpu.roll(xs, LANES - HD_C // 2, 1), pltpu.roll(xs, HD_C // 2, 1))
        outs.append(xs * cos + rot * sin)
    return outs[0] if len(outs) == 1 else jnp.concatenate(outs, axis=1)


def _sink_softmax_terms(s, valid, sink):
    s = jnp.where(valid, s * HD_C ** -0.5, -jnp.inf)
    mx = jnp.maximum(jnp.max(s, axis=1, keepdims=True), sink)
    e = jnp.exp(s - mx)
    return e, 1.0 / (jnp.sum(e, axis=1, keepdims=True) + jnp.exp(sink - mx))


def _dup_halves(x, lo):
    out = []
    for j in range(x.shape[1] // LANES):
        slab = x[:, j * LANES:(j + 1) * LANES]
        swapped = pltpu.roll(slab, HD_C, 1)
        out += [jnp.where(lo, slab, swapped), jnp.where(lo, swapped, slab)]
    return jnp.concatenate(out, axis=1)


def _swa_prompt_kernel(q_ref, k_ref, v_ref, cos_ref, sin_ref, sink_ref, o_ref, kk_ref, vk_ref, kb_ref, vb_ref):
    nblk = pl.program_id(1)
    w = WINDOW

    @pl.when(nblk == 0)
    def _():
        kb_ref[0:w, :] = jnp.zeros((w, kb_ref.shape[1]), BF16)
        vb_ref[0:w, :] = jnp.zeros((w, vb_ref.shape[1]), BF16)

    cos, sin = cos_ref[...], sin_ref[...]
    lane = lax.broadcasted_iota(jnp.int32, (w, LANES), 1)
    first_half = (lane & (HD_C - 1)) < HD_C // 2
    lo = lane < HD_C
    q = _rope(q_ref[...], cos, sin, first_half).astype(BF16)
    k = _rope(k_ref[...], cos, sin, first_half)
    v = v_ref[...]
    kk_ref[0] = k
    vk_ref[0] = v
    kb_ref[w:2 * w, :] = _dup_halves(k, lo).astype(BF16)
    vb_ref[w:2 * w, :] = _dup_halves(v, lo).astype(BF16)
    rows = GROUP * w
    qi = lax.broadcasted_iota(jnp.int32, (rows, 2 * w), 0) & (w - 1)
    kj = lax.broadcasted_iota(jnp.int32, (rows, 2 * w), 1)
    valid = (kj > qi) & (kj <= qi + w) & ((kj >= w) | (nblk > 0))
    zero = jnp.zeros((w, LANES), BF16)
    for g0 in range(0, HKV_C, SWA_STAGE_KV):
        gs = range(g0, g0 + SWA_STAGE_KV)
        sc, sinks = [], []
        for g in gs:
            slabs = [q[:, (GROUP // 2 * g + j) * LANES:(GROUP // 2 * g + j + 1) * LANES] for j in range(GROUP // 2)]
            q4 = jnp.concatenate([jnp.where(keep, s, zero) for s in slabs for keep in (lo, ~lo)], axis=0)
            sc.append(_dot_nt(q4, kb_ref[:, g * LANES:(g + 1) * LANES]))
            sinks.append(jnp.concatenate([jnp.broadcast_to(sink_ref[:, h:h + 1], (w, 1))
                                          for h in range(g * GROUP, (g + 1) * GROUP)], axis=0))
        terms = [_sink_softmax_terms(sc[i], valid, sinks[i]) for i in range(len(gs))]
        pv = [_dot(terms[i][0].astype(BF16), vb_ref[:, g * LANES:(g + 1) * LANES]) for i, g in enumerate(gs)]
        for i, g in enumerate(gs):
            o = pv[i] * terms[i][1]
            for j in range(GROUP // 2):
                pair = jnp.where(lo, o[2 * j * w:(2 * j + 1) * w, :], o[(2 * j + 1) * w:(2 * j + 2) * w, :])
                o_ref[:, (GROUP // 2 * g + j) * LANES:(GROUP // 2 * g + j + 1) * LANES] = pair.astype(o_ref.dtype)
    kb_ref[0:w, :] = kb_ref[w:2 * w, :]
    vb_ref[0:w, :] = vb_ref[w:2 * w, :]


def _swa_prompt(p, cos, sin, sinks, *, batch, seq):
    w = WINDOW
    nb = seq // w
    nq, nkv = HQ_C * HD_C, HKV_C * HD_C
    keep = pl.BlockSpec((1, w, nkv), lambda b, i: (b, 0, 0))
    tab = pl.BlockSpec((w, LANES), lambda b, i: (i, 0))
    return pl.pallas_call(
        _swa_prompt_kernel,
        grid=(batch, nb),
        in_specs=[pl.BlockSpec((w, nq), lambda b, i: (b * nb + i, 0)),
                  pl.BlockSpec((w, nkv), lambda b, i: (b * nb + i, nq // nkv)),
                  pl.BlockSpec((w, nkv), lambda b, i: (b * nb + i, nq // nkv + 1)),
                  tab, tab, pl.BlockSpec((1, HQ_C), lambda b, i: (0, 0))],
        out_specs=[pl.BlockSpec((w, nq), lambda b, i: (b * nb + i, 0)), keep, keep],
        out_shape=[jax.ShapeDtypeStruct((batch * seq, nq), BF16),
                   jax.ShapeDtypeStruct((batch, w, nkv), F32),
                   jax.ShapeDtypeStruct((batch, w, nkv), F32)],
        scratch_shapes=[pltpu.VMEM((2 * w, HKV_C * LANES), BF16), pltpu.VMEM((2 * w, HKV_C * LANES), BF16)],
        compiler_params=_params(("parallel", "arbitrary")),
        name="swa_prompt",
    )(p, p, p, cos, sin, sinks)


def _swa_sample_kernel(q_ref, k_ref, v_ref, kb_ref, vb_ref, cos_ref, sin_ref, sink_ref,
                       o_ref, kk_ref, vk_ref, kc_ref, vc_ref, *, seq):
    w = WINDOW
    span = 2 * w
    cos, sin = cos_ref[...], sin_ref[...]
    first_half = (lax.broadcasted_iota(jnp.int32, (seq, LANES), 1) & (HD_C - 1)) < HD_C // 2
    q = _rope(q_ref[...], cos, sin, first_half).astype(BF16)
    kc_ref[0:w, :] = kb_ref[0]
    vc_ref[0:w, :] = vb_ref[0]
    kc_ref[w:w + seq, :] = _rope(k_ref[...], cos, sin, first_half)
    vc_ref[w:w + seq, :] = v_ref[...]
    kc_ref[w + seq:span, :] = jnp.zeros((span - w - seq, kc_ref.shape[1]), F32)
    vc_ref[w + seq:span, :] = jnp.zeros((span - w - seq, vc_ref.shape[1]), F32)
    kk_ref[0] = kc_ref[seq:seq + w, :]
    vk_ref[0] = vc_ref[seq:seq + w, :]
    kcat, vcat = kc_ref[...].astype(BF16), vc_ref[...].astype(BF16)
    rows = GROUP * seq
    qi = lax.broadcasted_iota(jnp.int32, (rows, span), 0) & (seq - 1)
    kj = lax.broadcasted_iota(jnp.int32, (rows, span), 1)
    valid = (kj > qi) & (kj <= qi + w) & (kj + (PAST_LEN - w) >= 0)
    groups = range(HKV_C)
    heads = [range(g * GROUP, (g + 1) * GROUP) for g in groups]
    sc = [_dot_nt(jnp.concatenate([q[:, h * HD_C:(h + 1) * HD_C] for h in heads[g]], axis=0),
                  kcat[:, g * HD_C:(g + 1) * HD_C]) for g in groups]
    terms = []
    for g in groups:
        sink = jnp.concatenate([jnp.broadcast_to(sink_ref[:, h:h + 1], (seq, 1)) for h in heads[g]], axis=0)
        terms.append(_sink_softmax_terms(sc[g], valid, sink))
    pv = [_dot(terms[g][0].astype(BF16), vcat[:, g * HD_C:(g + 1) * HD_C]) for g in groups]
    for g in groups:
        o = pv[g] * terms[g][1]
        for hh, h in enumerate(heads[g]):
            o_ref[:, h * HD_C:(h + 1) * HD_C] = o[hh * seq:(hh + 1) * seq, :]


def _swa_sample(p, k_buf, v_buf, cos, sin, sinks, *, batch, seq, row0):
    w = WINDOW
    nq, nkv = HQ_C * HD_C, HKV_C * HD_C
    rb = row0 // seq
    keep = pl.BlockSpec((1, w, nkv), lambda b: (b, 0, 0))
    tab = pl.BlockSpec((seq, LANES), lambda b: (0, 0))
    return pl.pallas_call(
        functools.partial(_swa_sample_kernel, seq=seq),
        grid=(batch,),
        in_specs=[pl.BlockSpec((seq, nq), lambda b: (rb + b, 0)),
                  pl.BlockSpec((seq, nkv), lambda b: (rb + b, nq // nkv)),
                  pl.BlockSpec((seq, nkv), lambda b: (rb + b, nq // nkv + 1)),
                  keep, keep, tab, tab, pl.BlockSpec((1, HQ_C), lambda b: (0, 0))],
        out_specs=[pl.BlockSpec((seq, nq), lambda b: (b, 0)), keep, keep],
        out_shape=[jax.ShapeDtypeStruct((batch * seq, nq), F32),
                   jax.ShapeDtypeStruct((batch, w, nkv), F32),
                   jax.ShapeDtypeStruct((batch, w, nkv), F32)],
        scratch_shapes=[pltpu.VMEM((2 * w, nkv), F32), pltpu.VMEM((2 * w, nkv), F32)],
        compiler_params=_params(("parallel",)),
        name="swa_sample",
    )(p, p, p, k_buf, v_buf, cos, sin, sinks)


def _rope_tables(pos):
    half = HD_C // 2
    inv = ROPE_THETA ** (-jnp.arange(half, dtype=F32) / half)
    ang = pos.astype(F32)[:, None] * inv[None, :]
    reps = LANES // half
    return jnp.tile(jnp.cos(ang), (1, reps)), jnp.tile(jnp.sin(ang), (1, reps))


def kernel(x_prompt, x_sample, state_mlstm_C, state_mlstm_n, state_mlstm_m, state_gdn_S, state_gdn_conv,
           cache_swa_k, cache_swa_v, norm_w, ffn_w_gate, ffn_w_up, ffn_w_down, ab_w_in, ab_w_out,
           mlstm_b_i, mlstm_b_f, mlstm_norm_w, gdn_conv_w, gdn_a_log, gdn_dt_bias, gdn_norm_w,
           c_w_qkv, c_w_out, c_sinks):
    bp, lp, d = x_prompt.shape
    bs, ls, _ = x_sample.shape
    mp, ms = bp * lp, bs * ls
    x = jnp.concatenate([x_prompt.reshape(mp, d), x_sample.reshape(ms, d)], axis=0)

    wg = _cast_pad(ffn_w_gate.reshape(DEPTH * 2, d, FFN_DIM), d, FFN_DIM, split_rows=True)
    wu = _cast_pad(ffn_w_up.reshape(DEPTH * 2, d, FFN_DIM), d, FFN_DIM, split_rows=True)
    wd = _cast_pad(ffn_w_down.reshape(DEPTH * 2, FFN_DIM, d), FFN_DIM, d, split_rows=False)
    w_ab = _perm_ab(jnp.swapaxes(ab_w_in, 1, 2))
    w_ab_out = _cast_pad(ab_w_out, ab_w_out.shape[1], d, split_rows=True)
    w_c = _cast_pad(c_w_qkv, d, c_w_qkv.shape[2], split_rows=True)
    w_c_out = _cast_pad(c_w_out, c_w_out.shape[1], d, split_rows=True)

    cos_p, sin_p = _rope_tables(jnp.arange(lp, dtype=jnp.int32))
    cos_s, sin_s = _rope_tables(PAST_LEN + jnp.arange(ls, dtype=jnp.int32))
    nkv = HKV_C * HD_C

    nw = norm_w.reshape(DEPTH * 6, 1, d)
    n_even = mlstm_b_i.shape[0]
    zeros = lambda *shape: jnp.zeros((n_even,) + shape, F32)
    mlstm_par = (mlstm_b_i[:, None, :], mlstm_b_f[:, None, :], mlstm_norm_w)
    gdn_par = (gdn_conv_w, gdn_a_log[:, None, :], gdn_dt_bias[:, None, :], gdn_norm_w[:, None, :])
    rec_p, rec_s, win_p, win_s = [], [], [], []
    for layer in range(DEPTH):
        j = layer // 2
        x = _ffn(x, nw, wg, wu, wd, layer, 0)
        if layer % 2 == 0:
            p = _proj_in(x, nw, layer * 6 + 2, w_ab, j, w_transposed=True)
            prompt = dict(batch=bp, c=CHUNK, nc=lp // CHUNK, nseq=1, row0=0, out_dtype=BF16)
            sample = dict(batch=bs, c=ls, nc=1, nseq=SAMPLE_NSEQ, row0=mp, out_dtype=F32)
            ma_p, c1, n1, m1 = _mlstm(p, zeros(bp, HA, DK_A, DV_A), zeros(bp, HA, DK_A), zeros(bp, 1, HA),
                                      *mlstm_par, j, **prompt)
            mb_p, s1, cv1 = _gdn(p, zeros(bp, HB, DK_B, DV_B), zeros(bp, CONV_W - 1, GDN_QKV), *gdn_par, j, **prompt)
            rec_p.append((c1, n1, m1.reshape(bp, HA), s1, cv1))
            ma_s, c1, n1, m1 = _mlstm(p, state_mlstm_C, state_mlstm_n, state_mlstm_m[:, :, None, :],
                                      *mlstm_par, j, **sample)
            mb_s, s1, cv1 = _gdn(p, state_gdn_S, state_gdn_conv, *gdn_par, j, **sample)
            rec_s.append((c1, n1, m1.reshape(bs, HA), s1, cv1))
            x = _proj_out([ma_p, mb_p], [ma_s.astype(BF16), mb_s.astype(BF16)], w_ab_out, j, x, nw, layer * 6 + 3)
        else:
            p = _proj_in(x, nw, layer * 6 + 2, w_c, j)
            sinks = c_sinks[j][None, :]
            o_p, kk, vk = _swa_prompt(p, cos_p, sin_p, sinks, batch=bp, seq=lp)
            win_p.append((kk.reshape(bp, WINDOW, HKV_C, HD_C), vk.reshape(bp, WINDOW, HKV_C, HD_C)))
            o_s, kk, vk = _swa_sample(p, cache_swa_k[j].reshape(bs, WINDOW, nkv),
                                      cache_swa_v[j].reshape(bs, WINDOW, nkv), cos_s, sin_s, sinks,
                                      batch=bs, seq=ls, row0=mp)
            win_s.append((kk.reshape(bs, WINDOW, HKV_C, HD_C), vk.reshape(bs, WINDOW, HKV_C, HD_C)))
            x = _proj_out([o_p], [o_s.astype(BF16)], w_c_out, j, x, nw, layer * 6 + 3)
        x = _ffn(x, nw, wg, wu, wd, layer, 1, head_rows=mp if layer == DEPTH - 1 else None)

    stack = lambda sts, i: jnp.stack([st[i] for st in sts])
    y_prompt = x[0].reshape(bp, lp, d)
    y_sample = x[1].reshape(bs, ls, d)
    return ((y_prompt, y_sample)
            + tuple(stack(rec_p, i) for i in range(5)) + tuple(stack(win_p, i) for i in range(2))
            + tuple(stack(rec_s, i) for i in range(5)) + tuple(stack(win_s, i) for i in range(2)))
```

```python
import functools

import jax
import jax.numpy as jnp
from jax import lax
from jax.experimental import pallas as pl
from jax.experimental.pallas import tpu as pltpu

F32 = jnp.float32
BF16 = jnp.bfloat16

D_MODEL = 2048
DEPTH = 4
PAST_LEN = 16384
HA, DK_A, DV_A = 4, 128, 256
A_QK, A_V = HA * DK_A, HA * DV_A
GATE_CAP = 15.0
HB, DK_B, DV_B = 8, 128, 128
GDN_QKV, GDN_V = HB * (2 * DK_B + DV_B), HB * DV_B
CONV_W = 4
HQ_C, HKV_C, HD_C = 32, 8, 64
GROUP = HQ_C // HKV_C
WINDOW = 128
ROPE_THETA = 10000.0
FFN_DIM = 5504
EPS = 1e-6

LANES = 128
FFN_TM, FFN_TF = 768, 512
PROJ_TM, PROJ_TN = 768, 1536
CAST_TILE = 256
OUT_TM = 256
CHUNK_A, CHUNK_B = 256, 128
SAMPLE_NSEQ = 4
SWA_STAGE_KV = 2
OFF_AQ, OFF_AK, OFF_AV, OFF_AO = 0, A_QK, 2 * A_QK, 2 * A_QK + A_V
OFF_BQKV = 2 * A_QK + 2 * A_V
OFF_BZ = OFF_BQKV + GDN_QKV
OFF_GATES = OFF_BZ + GDN_V
IN_AB_PAD = 7680
VMEM_LIMIT = 56 * 1024 * 1024


def _rms(x, w):
    return x * lax.rsqrt(jnp.mean(x * x, axis=-1, keepdims=True) + EPS) * w


def _dot(a, b):
    return jnp.dot(a, b, preferred_element_type=F32)


def _dot_nt(a, b):
    return lax.dot_general(a, b, (((1,), (1,)), ((), ())), preferred_element_type=F32)


def _dot_tn(a, b):
    return lax.dot_general(a, b, (((0,), (0,)), ((), ())), preferred_element_type=F32)


def _params(sem):
    return pltpu.CompilerParams(dimension_semantics=sem, vmem_limit_bytes=VMEM_LIMIT)


def _cast_pad_kernel(x_ref, o_ref):
    r, n = x_ref.shape
    o_ref[:r, :n] = x_ref[...].astype(BF16)
    if o_ref.shape[1] > n:
        o_ref[:, n:] = jnp.zeros((o_ref.shape[0], o_ref.shape[1] - n), BF16)
    if o_ref.shape[0] > r:
        o_ref[r:, :] = jnp.zeros((o_ref.shape[0] - r, o_ref.shape[1]), BF16)


def _cast_pad(w, rows_out, cols_out, *, split_rows):
    l, r, n = w.shape
    if split_rows:
        assert rows_out == r and r % CAST_TILE == 0
        grid = (l, r // CAST_TILE)
        in_spec = pl.BlockSpec((None, CAST_TILE, n), lambda a, b: (a, b, 0))
        out_spec = pl.BlockSpec((None, CAST_TILE, cols_out), lambda a, b: (a, b, 0))
    else:
        assert cols_out == n and n % CAST_TILE == 0
        grid = (l, n // CAST_TILE)
        in_spec = pl.BlockSpec((None, r, CAST_TILE), lambda a, b: (a, 0, b))
        out_spec = pl.BlockSpec((None, rows_out, CAST_TILE), lambda a, b: (a, 0, b))
    return pl.pallas_call(
        _cast_pad_kernel, grid=grid, in_specs=[in_spec], out_specs=out_spec,
        out_shape=jax.ShapeDtypeStruct((l, rows_out, cols_out), BF16),
        compiler_params=_params(("parallel", "parallel")), name="cast_pad",
    )(w)


def _perm_ab_kernel(x_ref, o_ref):
    o_ai = OFF_BQKV
    o_bq = o_ai + 2 * HA
    o_bb = o_bq + GDN_QKV + GDN_V
    n_gates = 2 * HA + 2 * HB
    o_ref[:OFF_BQKV, :] = x_ref[:o_ai, :].astype(BF16)
    o_ref[OFF_BQKV:OFF_GATES, :] = x_ref[o_bq:o_bb, :].astype(BF16)
    o_ref[OFF_GATES:, :] = jnp.zeros((IN_AB_PAD - OFF_GATES, o_ref.shape[1]), BF16)
    o_ref[OFF_GATES:OFF_GATES + 2 * HA, :] = x_ref[o_ai:o_bq, :].astype(BF16)
    o_ref[OFF_GATES + 2 * HA:OFF_GATES + n_gates, :] = x_ref[o_bb:, :].astype(BF16)


def _perm_ab(wt):
    l, n, r = wt.shape
    return pl.pallas_call(
        _perm_ab_kernel, grid=(l, r // CAST_TILE),
        in_specs=[pl.BlockSpec((None, n, CAST_TILE), lambda a, b: (a, 0, b))],
        out_specs=pl.BlockSpec((None, IN_AB_PAD, CAST_TILE), lambda a, b: (a, 0, b)),
        out_shape=jax.ShapeDtypeStruct((l, IN_AB_PAD, r), BF16),
        compiler_params=_params(("parallel", "parallel")), name="perm_ab",
    )(wt)


def _ffn_kernel(*refs, nf_main, n_tail, tail_rows):
    x_ref, nwa_ref, nwb_ref, wg_ref, wu_ref, wd_ref = refs[:6]
    wgt, wut, wdt = (refs[6 + k * n_tail:6 + (k + 1) * n_tail] for k in range(3))
    rest = refs[6 + 3 * n_tail:]
    o_ref, h_ref = rest[0], rest[-1]
    f = pl.program_id(1)
    last = nf_main if n_tail else nf_main - 1

    def partial_out(wg, wu, wd):
        h = h_ref[...]
        g = _dot(h, wg)
        u = _dot(h, wu)
        return _dot((g * jax.nn.sigmoid(g) * u).astype(BF16), wd)

    def finish(acc):
        o_ref[...] = x_ref[...] + 0.5 * _rms(acc, nwb_ref[...])
        if tail_rows:
            @pl.when(pl.program_id(0) == pl.num_programs(0) - 1)
            def _():
                rest[1][...] = o_ref[o_ref.shape[0] - tail_rows:, :]

    @pl.when(f == 0)
    def _():
        h_ref[...] = _rms(x_ref[...], nwa_ref[...]).astype(BF16)
        o_ref[...] = partial_out(wg_ref[...], wu_ref[...], wd_ref[...])

    @pl.when((f > 0) & (f < last))
    def _():
        o_ref[...] += partial_out(wg_ref[...], wu_ref[...], wd_ref[...])

    @pl.when(f == last)
    def _():
        if n_tail:
            cat = lambda rs, axis: jnp.concatenate([r[...] for r in rs], axis=axis)
            finish(o_ref[...] + partial_out(cat(wgt, 1), cat(wut, 1), cat(wdt, 0)))
        else:
            finish(o_ref[...] + partial_out(wg_ref[...], wu_ref[...], wd_ref[...]))


def _ffn(x, nw, wg, wu, wd, layer, half, head_rows=None):
    m, d = x.shape
    fdim = wg.shape[-1]
    nf_main, n_tail = fdim // FFN_TF, (fdim % FFN_TF) // LANES
    assert nf_main >= 2 and fdim == nf_main * FFN_TF + n_tail * LANES
    pre = layer * 6 + 4 * half
    wi = layer * 2 + half
    row = pl.BlockSpec((FFN_TM, d), lambda i, f: (i, 0))
    vec = lambda r: pl.BlockSpec((None, 1, d), lambda i, f: (r, 0, 0))
    main = lambda f: jnp.minimum(f, nf_main - 1)
    wcol = pl.BlockSpec((None, d, FFN_TF), lambda i, f: (wi, 0, main(f)))
    wrow = pl.BlockSpec((None, FFN_TF, d), lambda i, f: (wi, main(f), 0))
    t0 = nf_main * FFN_TF // LANES
    tcol = [pl.BlockSpec((None, d, LANES), functools.partial(lambda i, f, t: (wi, 0, t0 + t), t=t)) for t in range(n_tail)]
    trow = [pl.BlockSpec((None, LANES, d), functools.partial(lambda i, f, t: (wi, t0 + t, 0), t=t)) for t in range(n_tail)]
    tail_rows = 0 if head_rows is None else m - head_rows
    assert tail_rows <= FFN_TM
    out_specs, out_shape = row, jax.ShapeDtypeStruct((m, d), F32)
    if tail_rows:
        out_specs = [row, pl.BlockSpec((tail_rows, d), lambda i, f: (0, 0))]
        out_shape = [jax.ShapeDtypeStruct((head_rows, d), F32), jax.ShapeDtypeStruct((tail_rows, d), F32)]
    return pl.pallas_call(
        functools.partial(_ffn_kernel, nf_main=nf_main, n_tail=n_tail, tail_rows=tail_rows),
        grid=(m // FFN_TM, nf_main + (1 if n_tail else 0)),
        in_specs=[row, vec(pre), vec(pre + 1), wcol, wcol, wrow] + tcol + tcol + trow,
        out_specs=out_specs,
        out_shape=out_shape,
        scratch_shapes=[pltpu.VMEM((FFN_TM, d), BF16)],
        compiler_params=_params(("parallel", "arbitrary")),
        name="ffn",
    )(x, nw, nw, wg, wu, wd, *([wg] * n_tail), *([wu] * n_tail), *([wd] * n_tail))


def _proj_in_kernel(x_ref, nw_ref, w_ref, o_ref, h_ref, *, w_transposed):
    mm = _dot_nt if w_transposed else _dot

    @pl.when(pl.program_id(1) == 0)
    def _():
        h_ref[...] = _rms(x_ref[...], nw_ref[...]).astype(BF16)
        o_ref[...] = mm(h_ref[...], w_ref[...])

    @pl.when(pl.program_id(1) != 0)
    def _():
        o_ref[...] = mm(h_ref[...], w_ref[...])


def _proj_in(x, nw, nw_row, w, j, w_transposed=False):
    m, d = x.shape
    n = w.shape[1] if w_transposed else w.shape[2]
    w_spec = (pl.BlockSpec((None, PROJ_TN, d), lambda i, k: (j, k, 0)) if w_transposed
              else pl.BlockSpec((None, d, PROJ_TN), lambda i, k: (j, 0, k)))
    return pl.pallas_call(
        functools.partial(_proj_in_kernel, w_transposed=w_transposed),
        grid=(m // PROJ_TM, n // PROJ_TN),
        in_specs=[pl.BlockSpec((PROJ_TM, d), lambda i, k: (i, 0)),
                  pl.BlockSpec((None, 1, d), lambda i, k: (nw_row, 0, 0)),
                  w_spec],
        out_specs=pl.BlockSpec((PROJ_TM, PROJ_TN), lambda i, k: (i, k)),
        out_shape=jax.ShapeDtypeStruct((m, n), F32),
        scratch_shapes=[pltpu.VMEM((PROJ_TM, d), BF16)],
        compiler_params=_params(("parallel", "arbitrary")),
        name="proj_in",
    )(x, nw, w)


def _proj_out_kernel(*refs, n_parts, prompt_tiles):
    p_refs, s_refs, w_refs = refs[:n_parts], refs[n_parts:2 * n_parts], refs[2 * n_parts:3 * n_parts]
    x_ref, nw_ref, o_ref = refs[3 * n_parts:]

    def emit(a_refs):
        y = _dot(a_refs[0][...], w_refs[0][...])
        for a_ref, w_ref in zip(a_refs[1:], w_refs[1:]):
            y += _dot(a_ref[...], w_ref[...])
        o_ref[...] = x_ref[...] + _rms(y, nw_ref[...])

    is_prompt = pl.program_id(0) < prompt_tiles
    pl.when(is_prompt)(lambda: emit(p_refs))
    pl.when(jnp.logical_not(is_prompt))(lambda: emit(s_refs))


def _proj_out(p_parts, s_parts, w, j, x, nw, nw_row):
    m, d = x.shape
    n_parts = len(p_parts)
    mp, ms = p_parts[0].shape[0], s_parts[0].shape[0]
    assert mp % OUT_TM == 0 and ms == OUT_TM and mp + ms == m
    prompt_tiles = mp // OUT_TM
    kw = [p.shape[1] for p in p_parts]
    assert len(set(kw)) == 1 and sum(kw) == w.shape[1]
    in_specs = [pl.BlockSpec((OUT_TM, k), lambda i: (jnp.minimum(i, prompt_tiles - 1), 0)) for k in kw]
    in_specs += [pl.BlockSpec((OUT_TM, k), lambda i: (0, 0)) for k in kw]
    in_specs += [pl.BlockSpec((None, k, d), functools.partial(lambda i, part: (j, part, 0), part=part))
                 for part, k in enumerate(kw)]
    in_specs += [pl.BlockSpec((OUT_TM, d), lambda i: (i, 0)), pl.BlockSpec((None, 1, d), lambda i: (nw_row, 0, 0))]
    return pl.pallas_call(
        functools.partial(_proj_out_kernel, n_parts=n_parts, prompt_tiles=prompt_tiles),
        grid=(m // OUT_TM,),
        in_specs=in_specs,
        out_specs=pl.BlockSpec((OUT_TM, d), lambda i: (i, 0)),
        out_shape=jax.ShapeDtypeStruct((m, d), F32),
        compiler_params=_params(("parallel",)),
        name="proj_out",
    )(*p_parts, *s_parts, *([w] * n_parts), x, nw)


def _iota2(c):
    return lax.broadcasted_iota(jnp.int32, (c, c), 0), lax.broadcasted_iota(jnp.int32, (c, c), 1)


def _split3(x):
    a = x.astype(BF16)
    r = x - a.astype(F32)
    b = r.astype(BF16)
    return a, b, (r - b.astype(F32)).astype(BF16)


def _mask_dot(mask01, x):
    a, b, d = _split3(x)
    return _dot(mask01, a) + _dot(mask01, b) + _dot(mask01, d)


def _mask_dot_t(x, mask01):
    a, b, d = _split3(x)
    return _dot_tn(a, mask01) + _dot_tn(b, mask01) + _dot_tn(d, mask01)


def _softcap(x, cap):
    return cap * jnp.tanh(x / cap)


def _mlstm_kernel(q_ref, k_ref, v_ref, og_ref, g_ref, c0_ref, n0_ref, m0_ref, bi_ref, bf_ref, nw_ref,
                  mix_ref, c1_ref, n1_ref, m1_ref, *, c, nseq):
    @pl.when(pl.program_id(1) == 0)
    def _():
        c1_ref[...] = c0_ref[...]
        n1_ref[...] = n0_ref[...]
        m1_ref[...] = m0_ref[...]

    row, col = _iota2(c)
    causal = col <= row
    eye01 = (row == col).astype(BF16)
    low01 = causal.astype(BF16)
    upp01 = (row <= col).astype(BF16)
    ig_cols, b_cols, b_rows, ig_rows = [], [], [], []
    for s in range(nseq):
        gates = g_ref[s * c:(s + 1) * c, :]
        ig = _softcap(gates[:, 0:HA] + bi_ref[...], GATE_CAP)
        lf = jax.nn.log_sigmoid(_softcap(gates[:, HA:2 * HA] + bf_ref[...], GATE_CAP))
        ig_cols.append(ig)
        b_cols.append(_mask_dot(low01, lf))
        b_rows.append(_mask_dot_t(lf, upp01))
        ig_rows.append(_mask_dot_t(ig, eye01))
    units = [(s, h) for s in range(nseq) for h in range(HA)]
    idx = range(len(units))
    rows = lambda s: slice(s * c, (s + 1) * c)
    q = [q_ref[rows(s), h * DK_A:(h + 1) * DK_A] * DK_A ** -0.5 for s, h in units]
    k = [k_ref[rows(s), h * DK_A:(h + 1) * DK_A] for s, h in units]
    q16 = [x.astype(BF16) for x in q]
    k16 = [x.astype(BF16) for x in k]
    v16 = [v_ref[rows(s), h * DV_A:(h + 1) * DV_A].astype(BF16) for s, h in units]
    cp = [c1_ref[s, h] for s, h in units]
    npv = [n1_ref[s, h:h + 1, :] for s, h in units]
    qk = [_dot_nt(q16[u], k16[u]) for u in idx]
    qc = [_dot(q16[u], cp[u].astype(BF16)) for u in idx]
    sc, sc16, kw, m, w_inter, decay = [], [], [], [], [], []
    for u, (s, h) in enumerate(units):
        b_col, ig_col = b_cols[s][:, h:h + 1], ig_cols[s][:, h:h + 1]
        dmat = jnp.where(causal, b_col - b_rows[s][h:h + 1, :] + ig_rows[s][h:h + 1, :], -jnp.inf)
        inter = b_col + m1_ref[s, :, h:h + 1]
        mh = jnp.maximum(inter, jnp.max(dmat, axis=1, keepdims=True))
        su = qk[u] * jnp.exp(dmat - mh)
        m_new = mh[c - 1:c, :]
        kw.append(k[u] * jnp.exp(b_col[c - 1:c, :] - b_col + ig_col - m_new))
        decay.append(jnp.exp(inter[c - 1:c, :] - m_new))
        w_inter.append(jnp.exp(inter - mh))
        m.append(mh)
        sc.append(su)
        sc16.append(su.astype(BF16))
    sv = [_dot(sc16[u], v16[u]) for u in idx]
    upd = [_dot_tn(kw[u].astype(BF16), v16[u]) for u in idx]
    for u, (s, h) in enumerate(units):
        num = w_inter[u] * qc[u] + sv[u]
        den = (w_inter[u] * jnp.sum(q[u] * npv[u], axis=1, keepdims=True)
               + jnp.sum(sc[u], axis=1, keepdims=True))
        hh = num / jnp.maximum(jnp.abs(den), jnp.exp(-m[u]))
        c1_ref[s, h] = decay[u] * cp[u] + upd[u]
        n1_ref[s, h:h + 1, :] = decay[u] * npv[u] + jnp.sum(kw[u], axis=0, keepdims=True)
        m1_ref[s, :, h:h + 1] = m[u][c - 1:c, :]
        y = _rms(hh, nw_ref[h:h + 1, :]) * jax.nn.sigmoid(og_ref[rows(s), h * DV_A:(h + 1) * DV_A])
        mix_ref[rows(s), h * DV_A:(h + 1) * DV_A] = y.astype(mix_ref.dtype)


def _seq_specs(c, nc, nseq, row0, j):
    assert nseq == 1 or nc == 1
    rb = row0 // (nseq * c)
    assert rb * nseq * c == row0
    rows = lambda width, off: pl.BlockSpec((nseq * c, width), lambda b, i: (rb + b * nc + i, off // width))
    out_rows = lambda width: pl.BlockSpec((nseq * c, width), lambda b, i: (b * nc + i, 0))
    state_in = lambda shape: pl.BlockSpec((None, nseq) + shape, lambda b, i: (j, b) + (0,) * len(shape))
    state_out = lambda shape: pl.BlockSpec((nseq,) + shape, lambda b, i: (b,) + (0,) * len(shape))
    small = lambda shape: pl.BlockSpec((None,) + shape, lambda b, i: (j,) + (0,) * len(shape))
    return rows, out_rows, state_in, state_out, small


def _mlstm(p, c0, n0, m0, b_i, b_f, norm_w, j, *, batch, c, nc, nseq, row0, out_dtype):
    rows, out_rows, state_in, state_out, small = _seq_specs(c, nc, nseq, row0, j)
    return pl.pallas_call(
        functools.partial(_mlstm_kernel, c=c, nseq=nseq),
        grid=(batch // nseq, nc),
        in_specs=[rows(A_QK, OFF_AQ), rows(A_QK, OFF_AK), rows(A_V, OFF_AV), rows(A_V, OFF_AO),
                  rows(LANES, OFF_GATES),
                  state_in((HA, DK_A, DV_A)), state_in((HA, DK_A)), state_in((1, HA)),
                  small((1, HA)), small((1, HA)), small((HA, DV_A))],
        out_specs=[out_rows(A_V), state_out((HA, DK_A, DV_A)), state_out((HA, DK_A)), state_out((1, HA))],
        out_shape=[jax.ShapeDtypeStruct((batch * nc * c, A_V), out_dtype),
                   jax.ShapeDtypeStruct((batch, HA, DK_A, DV_A), F32),
                   jax.ShapeDtypeStruct((batch, HA, DK_A), F32),
                   jax.ShapeDtypeStruct((batch, 1, HA), F32)],
        compiler_params=_params(("parallel", "arbitrary")),
        name="mlstm",
    )(p, p, p, p, p, c0, n0, m0, b_i, b_f, norm_w)


def _unit_lower_inverses_minus_eye(mats, c):
    row, col = _iota2(c)
    nb = min(c, 16)
    sh = nb.bit_length() - 1
    diag_blk = (row >> sh) == (col >> sh)
    mm = lambda x, y: _dot(x.astype(BF16), y.astype(BF16))
    p = [jnp.where(diag_blk, a, 0.0) for a in mats]
    n = [-x for x in p]
    k = 1
    while 2 * k < nb:
        p = [mm(x, x) for x in p]
        np_ = [mm(x, y) for x, y in zip(n, p)]
        n = [x + y + z for x, y, z in zip(n, p, np_)]
        k *= 2
    s = nb
    while s < c:
        sh = s.bit_length() - 1
        off_blk = ((row >> (sh + 1)) == (col >> (sh + 1))) & ((row >> sh) != (col >> sh))
        off = [jnp.where(off_blk, a, 0.0) for a in mats]
        x = [o + mm(y, o) for o, y in zip(off, n)]
        xn = [mm(y, z) for y, z in zip(x, n)]
        n = [z - (y + w) for z, y, w in zip(n, x, xn)]
        s *= 2
    return n


def _gdn_kernel(x_ref, z_ref, g_ref, s0_ref, conv0_ref, cw_ref, alog_ref, dtb_ref, nw_ref,
                mix_ref, s1_ref, conv1_ref, xp_ref, *, c, nseq):
    @pl.when(pl.program_id(1) == 0)
    def _():
        s1_ref[...] = s0_ref[...]
        xp_ref[:, 5:8, :] = conv0_ref[...]

    row, col = _iota2(c)
    incl = col <= row
    strict = col < row
    low01, upp01 = incl.astype(BF16), (row <= col).astype(BF16)
    qkv, beta_cols, gc_cols, gc_rows, egc_cols = [], [], [], [], []
    for s in range(nseq):
        xp_ref[s, 8:8 + c, :] = x_ref[s * c:(s + 1) * c, :]
        y = xp_ref[s, 5:5 + c, :] * cw_ref[0:1, :]
        for j in range(1, CONV_W):
            y = y + xp_ref[s, 5 + j:5 + j + c, :] * cw_ref[j:j + 1, :]
        qkv.append(y * jax.nn.sigmoid(y))
        tail = xp_ref[s, 5 + c:8 + c, :]
        conv1_ref[s] = tail
        xp_ref[s, 5:8, :] = tail
        gates = g_ref[s * c:(s + 1) * c, :]
        beta_cols.append(jax.nn.sigmoid(gates[:, 2 * HA:2 * HA + HB]))
        g_cols = -jnp.exp(alog_ref[...]) * jax.nn.softplus(gates[:, 2 * HA + HB:2 * HA + 2 * HB] + dtb_ref[...])
        gc_cols.append(_mask_dot(low01, g_cols))
        gc_rows.append(_mask_dot_t(g_cols, upp01))
        egc_cols.append(jnp.exp(gc_cols[s]))
    units = [(s, h) for s in range(nseq) for h in range(HB)]
    idx = range(len(units))
    q, k, k16, kbeta, rhs, decay = [], [], [], [], [], []
    for s, h in units:
        qh = qkv[s][:, h * DK_B:(h + 1) * DK_B]
        kh = qkv[s][:, HB * DK_B + h * DK_B:HB * DK_B + (h + 1) * DK_B]
        vh = qkv[s][:, 2 * HB * DK_B + h * DV_B:2 * HB * DK_B + (h + 1) * DV_B]
        qh = qh * lax.rsqrt(jnp.sum(qh * qh, axis=-1, keepdims=True) + EPS) * DK_B ** -0.5
        kh = kh * lax.rsqrt(jnp.sum(kh * kh, axis=-1, keepdims=True) + EPS)
        beta = beta_cols[s][:, h:h + 1]
        kb = kh * beta
        q.append(qh)
        k.append(kh)
        k16.append(kh.astype(BF16))
        kbeta.append(kb)
        rhs.append(jnp.concatenate([vh * beta, kb * egc_cols[s][:, h:h + 1]], axis=1))
        decay.append(jnp.where(incl, jnp.exp(jnp.where(incl, gc_cols[s][:, h:h + 1] - gc_rows[s][h:h + 1, :], 0.0)), 0.0))
    kk = [_dot_nt(kbeta[u].astype(BF16), k16[u]) for u in idx]
    qk = [_dot_nt(q[u].astype(BF16), k16[u]) for u in idx]
    n = _unit_lower_inverses_minus_eye([jnp.where(strict, kk[u] * decay[u], 0.0) for u in idx], c)
    sol = [rhs[u] + _dot(n[u].astype(BF16), rhs[u].astype(BF16)) for u in idx]
    st = [s1_ref[s, h] for s, h in units]
    st16 = [x.astype(BF16) for x in st]
    ws = [_dot(sol[u][:, DV_B:].astype(BF16), st16[u]) for u in idx]
    qs = [_dot((q[u] * egc_cols[s][:, h:h + 1]).astype(BF16), st16[u]) for u, (s, h) in enumerate(units)]
    v_new16 = [(sol[u][:, :DV_B] - ws[u]).astype(BF16) for u in idx]
    av = [_dot((qk[u] * decay[u]).astype(BF16), v_new16[u]) for u in idx]
    upd = []
    for u, (s, h) in enumerate(units):
        kd = k[u] * jnp.exp(gc_cols[s][c - 1:c, h:h + 1] - gc_cols[s][:, h:h + 1])
        upd.append(_dot_tn(kd.astype(BF16), v_new16[u]))
    for u, (s, h) in enumerate(units):
        s1_ref[s, h] = jnp.exp(gc_cols[s][c - 1:c, h:h + 1]) * st[u] + upd[u]
        zz = z_ref[s * c:(s + 1) * c, h * DV_B:(h + 1) * DV_B]
        yo = _rms(qs[u] + av[u], nw_ref[...]) * (zz * jax.nn.sigmoid(zz))
        mix_ref[s * c:(s + 1) * c, h * DV_B:(h + 1) * DV_B] = yo.astype(mix_ref.dtype)


def _gdn(p, s0, conv0, conv_w, a_log, dt_bias, norm_w, j, *, batch, c, nc, nseq, row0, out_dtype):
    rows, out_rows, state_in, state_out, small = _seq_specs(c, nc, nseq, row0, j)
    return pl.pallas_call(
        functools.partial(_gdn_kernel, c=c, nseq=nseq),
        grid=(batch // nseq, nc),
        in_specs=[rows(GDN_QKV, OFF_BQKV), rows(GDN_V, OFF_BZ), rows(LANES, OFF_GATES),
                  state_in((HB, DK_B, DV_B)), state_in((CONV_W - 1, GDN_QKV)),
                  small((CONV_W, GDN_QKV)), small((1, HB)), small((1, HB)), small((1, DV_B))],
        out_specs=[out_rows(GDN_V), state_out((HB, DK_B, DV_B)), state_out((CONV_W - 1, GDN_QKV))],
        out_shape=[jax.ShapeDtypeStruct((batch * nc * c, GDN_V), out_dtype),
                   jax.ShapeDtypeStruct((batch, HB, DK_B, DV_B), F32),
                   jax.ShapeDtypeStruct((batch, CONV_W - 1, GDN_QKV), F32)],
        scratch_shapes=[pltpu.VMEM((nseq, c + 8, GDN_QKV), F32)],
        compiler_params=_params(("parallel", "arbitrary")),
        name="gdn",
    )(p, p, p, s0, conv0, conv_w, a_log, dt_bias, norm_w)


def _rope(x, cos, sin, first_half):
    outs = []
    for j in range(x.shape[1] // LANES):
        xs = x[:, j * LANES:(j + 1) * LANES]
        rot = jnp.where(first_half, -pltpu.roll(xs, LANES - HD_C // 2, 1), pltpu.roll(xs, HD_C // 2, 1))
        outs.append(xs * cos + rot * sin)
    return outs[0] if len(outs) == 1 else jnp.concatenate(outs, axis=1)


def _sink_softmax_terms(s, valid, sink):
    s = jnp.where(valid, s * HD_C ** -0.5, -jnp.inf)
    mx = jnp.maximum(jnp.max(s, axis=1, keepdims=True), sink)
    e = jnp.exp(s - mx)
    return e, 1.0 / (jnp.sum(e, axis=1, keepdims=True) + jnp.exp(sink - mx))


def _dup_halves(x, lo):
    out = []
    for j in range(x.shape[1] // LANES):
        slab = x[:, j * LANES:(j + 1) * LANES]
        swapped = pltpu.roll(slab, HD_C, 1)
        out += [jnp.where(lo, slab, swapped), jnp.where(lo, swapped, slab)]
    return jnp.concatenate(out, axis=1)


def _swa_prompt_kernel(q_ref, k_ref, v_ref, cos_ref, sin_ref, sink_ref, o_ref, kk_ref, vk_ref, kb_ref, vb_ref):
    nblk = pl.program_id(1)
    w = WINDOW

    @pl.when(nblk == 0)
    def _():
        kb_ref[0:w, :] = jnp.zeros((w, kb_ref.shape[1]), BF16)
        vb_ref[0:w, :] = jnp.zeros((w, vb_ref.shape[1]), BF16)

    cos, sin = cos_ref[...], sin_ref[...]
    lane = lax.broadcasted_iota(jnp.int32, (w, LANES), 1)
    first_half = (lane & (HD_C - 1)) < HD_C // 2
    lo = lane < HD_C
    q = _rope(q_ref[...], cos, sin, first_half).astype(BF16)
    k = _rope(k_ref[...], cos, sin, first_half)
    v = v_ref[...]
    kk_ref[0] = k
    vk_ref[0] = v
    kb_ref[w:2 * w, :] = _dup_halves(k, lo).astype(BF16)
    vb_ref[w:2 * w, :] = _dup_halves(v, lo).astype(BF16)
    rows = GROUP * w
    qi = lax.broadcasted_iota(jnp.int32, (rows, 2 * w), 0) & (w - 1)
    kj = lax.broadcasted_iota(jnp.int32, (rows, 2 * w), 1)
    valid = (kj > qi) & (kj <= qi + w) & ((kj >= w) | (nblk > 0))
    zero = jnp.zeros((w, LANES), BF16)
    for g0 in range(0, HKV_C, SWA_STAGE_KV):
        gs = range(g0, g0 + SWA_STAGE_KV)
        sc, sinks = [], []
        for g in gs:
            slabs = [q[:, (GROUP // 2 * g + j) * LANES:(GROUP // 2 * g + j + 1) * LANES] for j in range(GROUP // 2)]
            q4 = jnp.concatenate([jnp.where(keep, s, zero) for s in slabs for keep in (lo, ~lo)], axis=0)
            sc.append(_dot_nt(q4, kb_ref[:, g * LANES:(g + 1) * LANES]))
            sinks.append(jnp.concatenate([jnp.broadcast_to(sink_ref[:, h:h + 1], (w, 1))
                                          for h in range(g * GROUP, (g + 1) * GROUP)], axis=0))
        terms = [_sink_softmax_terms(sc[i], valid, sinks[i]) for i in range(len(gs))]
        pv = [_dot(terms[i][0].astype(BF16), vb_ref[:, g * LANES:(g + 1) * LANES]) for i, g in enumerate(gs)]
        for i, g in enumerate(gs):
            o = pv[i] * terms[i][1]
            for j in range(GROUP // 2):
                pair = jnp.where(lo, o[2 * j * w:(2 * j + 1) * w, :], o[(2 * j + 1) * w:(2 * j + 2) * w, :])
                o_ref[:, (GROUP // 2 * g + j) * LANES:(GROUP // 2 * g + j + 1) * LANES] = pair.astype(o_ref.dtype)
    kb_ref[0:w, :] = kb_ref[w:2 * w, :]
    vb_ref[0:w, :] = vb_ref[w:2 * w, :]


def _swa_prompt(p, cos, sin, sinks, *, batch, seq):
    w = WINDOW
    nb = seq // w
    nq, nkv = HQ_C * HD_C, HKV_C * HD_C
    keep = pl.BlockSpec((1, w, nkv), lambda b, i: (b, 0, 0))
    tab = pl.BlockSpec((w, LANES), lambda b, i: (i, 0))
    return pl.pallas_call(
        _swa_prompt_kernel,
        grid=(batch, nb),
        in_specs=[pl.BlockSpec((w, nq), lambda b, i: (b * nb + i, 0)),
                  pl.BlockSpec((w, nkv), lambda b, i: (b * nb + i, nq // nkv)),
                  pl.BlockSpec((w, nkv), lambda b, i: (b * nb + i, nq // nkv + 1)),
                  tab, tab, pl.BlockSpec((1, HQ_C), lambda b, i: (0, 0))],
        out_specs=[pl.BlockSpec((w, nq), lambda b, i: (b * nb + i, 0)), keep, keep],
        out_shape=[jax.ShapeDtypeStruct((batch * seq, nq), BF16),
                   jax.ShapeDtypeStruct((batch, w, nkv), F32),
                   jax.ShapeDtypeStruct((batch, w, nkv), F32)],
        scratch_shapes=[pltpu.VMEM((2 * w, HKV_C * LANES), BF16), pltpu.VMEM((2 * w, HKV_C * LANES), BF16)],
        compiler_params=_params(("parallel", "arbitrary")),
        name="swa_prompt",
    )(p, p, p, cos, sin, sinks)


def _swa_sample_kernel(q_ref, k_ref, v_ref, kb_ref, vb_ref, cos_ref, sin_ref, sink_ref,
                       o_ref, kk_ref, vk_ref, kc_ref, vc_ref, *, seq):
    w = WINDOW
    span = 2 * w
    cos, sin = cos_ref[...], sin_ref[...]
    first_half = (lax.broadcasted_iota(jnp.int32, (seq, LANES), 1) & (HD_C - 1)) < HD_C // 2
    q = _rope(q_ref[...], cos, sin, first_half).astype(BF16)
    kc_ref[0:w, :] = kb_ref[0]
    vc_ref[0:w, :] = vb_ref[0]
    kc_ref[w:w + seq, :] = _rope(k_ref[...], cos, sin, first_half)
    vc_ref[w:w + seq, :] = v_ref[...]
    kc_ref[w + seq:span, :] = jnp.zeros((span - w - seq, kc_ref.shape[1]), F32)
    vc_ref[w + seq:span, :] = jnp.zeros((span - w - seq, vc_ref.shape[1]), F32)
    kk_ref[0] = kc_ref[seq:seq + w, :]
    vk_ref[0] = vc_ref[seq:seq + w, :]
    kcat, vcat = kc_ref[...].astype(BF16), vc_ref[...].astype(BF16)
    rows = GROUP * seq
    qi = lax.broadcasted_iota(jnp.int32, (rows, span), 0) & (seq - 1)
    kj = lax.broadcasted_iota(jnp.int32, (rows, span), 1)
    valid = (kj > qi) & (kj <= qi + w) & (kj + (PAST_LEN - w) >= 0)
    groups = range(HKV_C)
    heads = [range(g * GROUP, (g + 1) * GROUP) for g in groups]
    sc = [_dot_nt(jnp.concatenate([q[:, h * HD_C:(h + 1) * HD_C] for h in heads[g]], axis=0),
                  kcat[:, g * HD_C:(g + 1) * HD_C]) for g in groups]
    terms = []
    for g in groups:
        sink = jnp.concatenate([jnp.broadcast_to(sink_ref[:, h:h + 1], (seq, 1)) for h in heads[g]], axis=0)
        terms.append(_sink_softmax_terms(sc[g], valid, sink))
    pv = [_dot(terms[g][0].astype(BF16), vcat[:, g * HD_C:(g + 1) * HD_C]) for g in groups]
    for g in groups:
        o = pv[g] * terms[g][1]
        for hh, h in enumerate(heads[g]):
            o_ref[:, h * HD_C:(h + 1) * HD_C] = o[hh * seq:(hh + 1) * seq, :]


def _swa_sample(p, k_buf, v_buf, cos, sin, sinks, *, batch, seq, row0):
    w = WINDOW
    nq, nkv = HQ_C * HD_C, HKV_C * HD_C
    rb = row0 // seq
    keep = pl.BlockSpec((1, w, nkv), lambda b: (b, 0, 0))
    tab = pl.BlockSpec((seq, LANES), lambda b: (0, 0))
    return pl.pallas_call(
        functools.partial(_swa_sample_kernel, seq=seq),
        grid=(batch,),
        in_specs=[pl.BlockSpec((seq, nq), lambda b: (rb + b, 0)),
                  pl.BlockSpec((seq, nkv), lambda b: (rb + b, nq // nkv)),
                  pl.BlockSpec((seq, nkv), lambda b: (rb + b, nq // nkv + 1)),
                  keep, keep, tab, tab, pl.BlockSpec((1, HQ_C), lambda b: (0, 0))],
        out_specs=[pl.BlockSpec((seq, nq), lambda b: (b, 0)), keep, keep],
        out_shape=[jax.ShapeDtypeStruct((batch * seq, nq), F32),
                   jax.ShapeDtypeStruct((batch, w, nkv), F32),
                   jax.ShapeDtypeStruct((batch, w, nkv), F32)],
        scratch_shapes=[pltpu.VMEM((2 * w, nkv), F32), pltpu.VMEM((2 * w, nkv), F32)],
        compiler_params=_params(("parallel",)),
        name="swa_sample",
    )(p, p, p, k_buf, v_buf, cos, sin, sinks)


def _rope_tables(pos):
    half = HD_C // 2
    inv = ROPE_THETA ** (-jnp.arange(half, dtype=F32) / half)
    ang = pos.astype(F32)[:, None] * inv[None, :]
    reps = LANES // half
    return jnp.tile(jnp.cos(ang), (1, reps)), jnp.tile(jnp.sin(ang), (1, reps))


def kernel(x_prompt, x_sample, state_mlstm_C, state_mlstm_n, state_mlstm_m, state_gdn_S, state_gdn_conv,
           cache_swa_k, cache_swa_v, norm_w, ffn_w_gate, ffn_w_up, ffn_w_down, ab_w_in, ab_w_out,
           mlstm_b_i, mlstm_b_f, mlstm_norm_w, gdn_conv_w, gdn_a_log, gdn_dt_bias, gdn_norm_w,
           c_w_qkv, c_w_out, c_sinks):
    bp, lp, d = x_prompt.shape
    bs, ls, _ = x_sample.shape
    mp, ms = bp * lp, bs * ls
    x = jnp.concatenate([x_prompt.reshape(mp, d), x_sample.reshape(ms, d)], axis=0)

    wg = _cast_pad(ffn_w_gate.reshape(DEPTH * 2, d, FFN_DIM), d, FFN_DIM, split_rows=True)
    wu = _cast_pad(ffn_w_up.reshape(DEPTH * 2, d, FFN_DIM), d, FFN_DIM, split_rows=True)
    wd = _cast_pad(ffn_w_down.reshape(DEPTH * 2, FFN_DIM, d), FFN_DIM, d, split_rows=False)
    w_ab = _perm_ab(jnp.swapaxes(ab_w_in, 1, 2))
    w_ab_out = _cast_pad(ab_w_out, ab_w_out.shape[1], d, split_rows=True)
    w_c = _cast_pad(c_w_qkv, d, c_w_qkv.shape[2], split_rows=True)
    w_c_out = _cast_pad(c_w_out, c_w_out.shape[1], d, split_rows=True)

    cos_p, sin_p = _rope_tables(jnp.arange(lp, dtype=jnp.int32))
    cos_s, sin_s = _rope_tables(PAST_LEN + jnp.arange(ls, dtype=jnp.int32))
    nkv = HKV_C * HD_C

    nw = norm_w.reshape(DEPTH * 6, 1, d)
    n_even = mlstm_b_i.shape[0]
    zeros = lambda *shape: jnp.zeros((n_even,) + shape, F32)
    mlstm_par = (mlstm_b_i[:, None, :], mlstm_b_f[:, None, :], mlstm_norm_w)
    gdn_par = (gdn_conv_w, gdn_a_log[:, None, :], gdn_dt_bias[:, None, :], gdn_norm_w[:, None, :])
    rec_p, rec_s, win_p, win_s = [], [], [], []
    for layer in range(DEPTH):
        j = layer // 2
        x = _ffn(x, nw, wg, wu, wd, layer, 0)
        if layer % 2 == 0:
            p = _proj_in(x, nw, layer * 6 + 2, w_ab, j, w_transposed=True)
            prompt = lambda c: dict(batch=bp, c=c, nc=lp // c, nseq=1, row0=0, out_dtype=BF16)
            sample = dict(batch=bs, c=ls, nc=1, nseq=SAMPLE_NSEQ, row0=mp, out_dtype=F32)
            ma_p, c1, n1, m1 = _mlstm(p, zeros(bp, HA, DK_A, DV_A), zeros(bp, HA, DK_A), zeros(bp, 1, HA),
                                      *mlstm_par, j, **prompt(CHUNK_A))
            mb_p, s1, cv1 = _gdn(p, zeros(bp, HB, DK_B, DV_B), zeros(bp, CONV_W - 1, GDN_QKV), *gdn_par, j,
                                 **prompt(CHUNK_B))
            rec_p.append((c1, n1, m1.reshape(bp, HA), s1, cv1))
            ma_s, c1, n1, m1 = _mlstm(p, state_mlstm_C, state_mlstm_n, state_mlstm_m[:, :, None, :],
                                      *mlstm_par, j, **sample)
            mb_s, s1, cv1 = _gdn(p, state_gdn_S, state_gdn_conv, *gdn_par, j, **sample)
            rec_s.append((c1, n1, m1.reshape(bs, HA), s1, cv1))
            x = _proj_out([ma_p, mb_p], [ma_s.astype(BF16), mb_s.astype(BF16)], w_ab_out, j, x, nw, layer * 6 + 3)
        else:
            p = _proj_in(x, nw, layer * 6 + 2, w_c, j)
            sinks = c_sinks[j][None, :]
            o_p, kk, vk = _swa_prompt(p, cos_p, sin_p, sinks, batch=bp, seq=lp)
            win_p.append((kk.reshape(bp, WINDOW, HKV_C, HD_C), vk.reshape(bp, WINDOW, HKV_C, HD_C)))
            o_s, kk, vk = _swa_sample(p, cache_swa_k[j].reshape(bs, WINDOW, nkv),
                                      cache_swa_v[j].reshape(bs, WINDOW, nkv), cos_s, sin_s, sinks,
                                      batch=bs, seq=ls, row0=mp)
            win_s.append((kk.reshape(bs, WINDOW, HKV_C, HD_C), vk.reshape(bs, WINDOW, HKV_C, HD_C)))
            x = _proj_out([o_p], [o_s.astype(BF16)], w_c_out, j, x, nw, layer * 6 + 3)
        x = _ffn(x, nw, wg, wu, wd, layer, 1, head_rows=mp if layer == DEPTH - 1 else None)

    stack = lambda sts, i: jnp.stack([st[i] for st in sts])
    y_prompt = x[0].reshape(bp, lp, d)
    y_sample = x[1].reshape(bs, ls, d)
    return ((y_prompt, y_sample)
            + tuple(stack(rec_p, i) for i in range(5)) + tuple(stack(win_p, i) for i in range(2))
            + tuple(stack(rec_s, i) for i in range(5)) + tuple(stack(win_s, i) for i in range(2)))
```
